```python
import math
import jax
import jax.numpy as jnp
from jax import lax
import numpy as np

D_MODEL = 2048
BATCH = 4
SEQ = 2048
DEPTH = 4
DEC_BATCH = 128
DEC_SEQ = 4
PAST_LEN = 16384
PAGE_SIZE = 128

N_MIXERS = 3
N_RG_LAYERS = (DEPTH + 2) // 3
N_GDN_LAYERS = (DEPTH + 1) // 3
N_SSD_LAYERS = DEPTH // 3
CONV_K = 4

RG_BLOCK = 256
D_RNN = (4 * D_MODEL // 3) // RG_BLOCK * RG_BLOCK
RG_BLOCKS = D_RNN // RG_BLOCK
RG_C = 8.0

GDN_QK_HEADS = 16
GDN_V_HEADS = 32
GDN_HEAD_K = 128
GDN_HEAD_V = 128
GDN_QK_DIM = GDN_QK_HEADS * GDN_HEAD_K
GDN_V_DIM = GDN_V_HEADS * GDN_HEAD_V
GDN_CONV_DIM = 2 * GDN_QK_DIM + GDN_V_DIM
GDN_IN = GDN_CONV_DIM + GDN_V_DIM + 2 * GDN_V_HEADS
GDN_CHUNK = 64

SSD_D_INNER = 2 * D_MODEL
SSD_HEAD_DIM = 64
SSD_HEADS = SSD_D_INNER // SSD_HEAD_DIM
SSD_GROUPS = 8
SSD_HPG = SSD_HEADS // SSD_GROUPS
SSD_STATE = 128
SSD_CONV_DIM = SSD_D_INNER + 2 * SSD_GROUPS * SSD_STATE
SSD_IN = SSD_D_INNER + SSD_CONV_DIM + SSD_HEADS
SSD_CHUNK = 128

MOE_GROUPS = 8
MOE_PER_GROUP = 8
MOE_EXPERTS = MOE_GROUPS * MOE_PER_GROUP
MOE_TOP_K = 2
MOE_D_FF = D_MODEL // 4

DN_ALPHA = (2.0 * DEPTH) ** 0.25
DN_BETA = (8.0 * DEPTH) ** -0.25
LN_EPS = 1e-5
RMS_EPS = 1e-6

kernel_name = 'hybrid_rglru_gdn_ssd_hmoe_step'


def _layer_norm(x, g, b):
    xf = x.astype(jnp.float32)
    mu = jnp.mean(xf, axis=-1, keepdims=True)
    xc = xf - mu
    var = jnp.mean(xc * xc, axis=-1, keepdims=True)
    return (xc * lax.rsqrt(var + LN_EPS) * g + b).astype(x.dtype)


def _rms_norm(x, w):
    xf = x.astype(jnp.float32)
    return xf * lax.rsqrt(jnp.mean(xf * xf, axis=-1, keepdims=True) + RMS_EPS) * w


def _l2norm(x):
    xf = x.astype(jnp.float32)
    return xf * lax.rsqrt(jnp.sum(xf * xf, axis=-1, keepdims=True) + RMS_EPS)


def _causal_conv(u, buf, w):
    T = u.shape[1]
    up = jnp.concatenate([buf.astype(u.dtype), u], axis=1)
    y = sum(up[:, k:k + T] * w[k] for k in range(CONV_K))
    return y, up[:, T:]


def _linear_combine(c1, c2):
    a1, b1 = c1
    a2, b2 = c2
    return a1 * a2, a2 * b1 + b2


def _rglru_mixer(x, h0, buf, w_in, conv_w, conv_b, w_r, b_r, w_i, b_i, lam, w_out):
    B, T, _ = x.shape
    gate, u = jnp.split(x @ w_in, 2, axis=-1)
    u, new_buf = _causal_conv(u, buf, conv_w)
    u = u + conv_b
    ub = u.reshape(B, T, RG_BLOCKS, RG_BLOCK)
    r = jax.nn.sigmoid(jnp.einsum('btnc,ncd->btnd', ub, w_r).reshape(B, T, D_RNN) + b_r)
    i = jax.nn.sigmoid(jnp.einsum('btnc,ncd->btnd', ub, w_i).reshape(B, T, D_RNN) + b_i)
    log_a = -RG_C * r.astype(jnp.float32) * jax.nn.softplus(-lam.astype(jnp.float32))
    a = jnp.exp(log_a)
    b = jnp.sqrt(-jnp.expm1(2.0 * log_a)) * (i * u).astype(jnp.float32)
    a_cum, b_cum = lax.associative_scan(_linear_combine, (a, b), axis=1)
    h = a_cum * h0.astype(jnp.float32)[:, None] + b_cum
    y = (jax.nn.gelu(gate) * h.astype(x.dtype)) @ w_out
    return y, h[:, -1].astype(h0.dtype), new_buf.astype(buf.dtype)


def _to_chunks(t, L):
    B, T = t.shape[:2]
    t = t.reshape((B, T // L, L) + t.shape[2:])
    return jnp.moveaxis(t, 2, 3)


def _gated_delta_chunked(q, k, v, beta, g, S0):
    B, T, H, _ = q.shape
    L = math.gcd(T, GDN_CHUNK)
    q, k, v, beta, g = (_to_chunks(t, L) for t in (q, k, v, beta, g))
    G = jnp.cumsum(g, axis=-1)
    causal = jnp.tril(jnp.ones((L, L), bool))
    strict = jnp.tril(jnp.ones((L, L), bool), -1)
    diff = G[..., :, None] - G[..., None, :]
    decay = jnp.where(causal, jnp.exp(jnp.where(causal, diff, 0.0)), 0.0)
    kk = jnp.einsum('bnhlk,bnhsk->bnhls', k, k)
    a_mat = jnp.where(strict, beta[..., None] * kk * decay, 0.0) + jnp.eye(L, dtype=jnp.float32)
    w = lax.linalg.triangular_solve(a_mat, k * (beta * jnp.exp(G))[..., None], left_side=True, lower=True)
    u0 = lax.linalg.triangular_solve(a_mat, v * beta[..., None], left_side=True, lower=True)
    qk = jnp.einsum('bnhlk,bnhsk->bnhls', q, k) * decay
    q_dec = q * jnp.exp(G)[..., None]
    k_dec = k * jnp.exp(G[..., -1:] - G)[..., None]
    g_end = jnp.exp(G[..., -1])

    def step(S, c):
        w_c, u0_c, qk_c, q_c, k_c, ge_c = c
        u = u0_c - jnp.einsum('bhlk,bhkv->bhlv', w_c, S)
        o = jnp.einsum('bhlk,bhkv->bhlv', q_c, S) + jnp.einsum('bhls,bhsv->bhlv', qk_c, u)
        S = ge_c[..., None, None] * S + jnp.einsum('bhlk,bhlv->bhkv', k_c, u)
        return S, o

    xs = tuple(jnp.moveaxis(t, 1, 0) for t in (w, u0, qk, q_dec, k_dec, g_end))
    S, o = lax.scan(step, S0, xs)
    o = jnp.swapaxes(jnp.moveaxis(o, 0, 1), 2, 3).reshape(B, T, H, -1)
    return o, S


def _gdn_mixer(x, S0, buf, w_in, conv_w, a_log, dt_bias, norm_w, w_out):
    B, T, _ = x.shape
    qkv, z, b, a = jnp.split(x @ w_in, [GDN_CONV_DIM, GDN_CONV_DIM + GDN_V_DIM, GDN_CONV_DIM + GDN_V_DIM + GDN_V_HEADS], axis=-1)
    qkv, new_buf = _causal_conv(qkv, buf, conv_w)
    qkv = jax.nn.silu(qkv)
    q, k, v = jnp.split(qkv, [GDN_QK_DIM, 2 * GDN_QK_DIM], axis=-1)
    rep = GDN_V_HEADS // GDN_QK_HEADS
    q = jnp.repeat(_l2norm(q.reshape(B, T, GDN_QK_HEADS, GDN_HEAD_K)) * GDN_HEAD_K ** -0.5, rep, axis=2)
    k = jnp.repeat(_l2norm(k.reshape(B, T, GDN_QK_HEADS, GDN_HEAD_K)), rep, axis=2)
    v = v.reshape(B, T, GDN_V_HEADS, GDN_HEAD_V).astype(jnp.float32)
    beta = jax.nn.sigmoid(b.astype(jnp.float32))
    g = -jnp.exp(a_log.astype(jnp.float32)) * jax.nn.softplus(a.astype(jnp.float32) + dt_bias.astype(jnp.float32))
    o, S = _gated_delta_chunked(q, k, v, beta, g, S0.astype(jnp.float32))
    o = _rms_norm(o, norm_w) * jax.nn.silu(z.reshape(B, T, GDN_V_HEADS, GDN_HEAD_V).astype(jnp.float32))
    y = o.reshape(B, T, GDN_V_DIM).astype(x.dtype) @ w_out
    return y, S.astype(S0.dtype), new_buf.astype(buf.dtype)


def _ssd_chunked(x, dt, a, bm, cm, h0):
    B, T = x.shape[:2]
    L = math.gcd(T, SSD_CHUNK)
    N = T // L

    def ch(t):
        return t.reshape((B, N, L) + t.shape[2:])

    x, dt, a, bm, cm = ch(x), ch(dt), ch(a), ch(bm), ch(cm)
    acum = jnp.cumsum(a, axis=2)
    causal = jnp.tril(jnp.ones((L, L), bool))[:, :, None, None]
    diff = acum[:, :, :, None] - acum[:, :, None, :]
    decay = jnp.where(causal, jnp.exp(jnp.where(causal, diff, 0.0)), 0.0)
    cb = jnp.einsum('bnlgs,bnmgs->bnlmg', cm, bm)
    xdt = x * dt[..., None]
    y_diag = jnp.einsum('bnlmgh,bnmghp->bnlghp', cb[..., None] * decay, xdt)
    to_end = jnp.exp(acum[:, :, -1:] - acum)
    states = jnp.einsum('bnlgs,bnlghp->bnghps', bm, xdt * to_end[..., None])
    chunk_decay = jnp.exp(acum[:, :, -1])

    def step(h, c):
        st, dec = c
        return dec[..., None, None] * h + st, h

    h_end, h_start = lax.scan(step, h0, (jnp.moveaxis(states, 1, 0), jnp.moveaxis(chunk_decay, 1, 0)))
    h_start = jnp.moveaxis(h_start, 0, 1)
    y_off = jnp.einsum('bnlgs,bnghps->bnlghp', cm, h_start) * jnp.exp(acum)[..., None]
    y = (y_diag + y_off).reshape((B, T) + x.shape[3:])
    return y, h_end


def _ssd_mixer(x, h0, buf, w_in, conv_w, conv_b, a_log, dt_bias, d_skip, norm_w, w_out):
    B, T, _ = x.shape
    z, xbc, dt = jnp.split(x @ w_in, [SSD_D_INNER, SSD_D_INNER + SSD_CONV_DIM], axis=-1)
    xbc, new_buf = _causal_conv(xbc, buf, conv_w)
    xbc = jax.nn.silu(xbc + conv_b).astype(jnp.float32)
    xs, bm, cm = jnp.split(xbc, [SSD_D_INNER, SSD_D_INNER + SSD_GROUPS * SSD_STATE], axis=-1)
    xs = xs.reshape(B, T, SSD_GROUPS, SSD_HPG, SSD_HEAD_DIM)
    bm = bm.reshape(B, T, SSD_GROUPS, SSD_STATE)
    cm = cm.reshape(B, T, SSD_GROUPS, SSD_STATE)
    dt = jax.nn.softplus(dt.astype(jnp.float32) + dt_bias.astype(jnp.float32)).reshape(B, T, SSD_GROUPS, SSD_HPG)
    a = dt * (-jnp.exp(a_log.astype(jnp.float32))).reshape(SSD_GROUPS, SSD_HPG)
    h0g = h0.astype(jnp.float32).reshape(B, SSD_GROUPS, SSD_HPG, SSD_HEAD_DIM, SSD_STATE)
    y, h = _ssd_chunked(xs, dt, a, bm, cm, h0g)
    y = y + d_skip.astype(jnp.float32).reshape(SSD_GROUPS, SSD_HPG)[:, :, None] * xs
    y = y.reshape(B, T, SSD_D_INNER) * jax.nn.silu(z.astype(jnp.float32))
    y = _rms_norm(y.reshape(B, T, SSD_GROUPS, -1), norm_w.reshape(SSD_GROUPS, -1)).reshape(B, T, SSD_D_INNER)
    out = y.astype(x.dtype) @ w_out
    return out, h.reshape(B, SSD_HEADS, SSD_HEAD_DIM, SSD_STATE).astype(h0.dtype), new_buf.astype(buf.dtype)


def _moe(x, wr_g, br_g, wr_e, br_e, w_gate, w_up, w_down):
    B, T, D = x.shape
    xt = x.reshape(-1, D)
    gprob = jax.nn.softmax((xt @ wr_g + br_g).astype(jnp.float32), axis=-1)
    gp, gi = lax.top_k(gprob, 1)
    g_onehot = jax.nn.one_hot(gi[:, 0], MOE_GROUPS, dtype=jnp.float32)
    elog = (xt @ wr_e + br_e).astype(jnp.float32).reshape(-1, MOE_GROUPS, MOE_PER_GROUP)
    eprob = jax.nn.softmax(jnp.einsum('ng,nge->ne', g_onehot, elog), axis=-1)
    ep, ei = lax.top_k(eprob, MOE_TOP_K)
    ew = ep / jnp.sum(ep, axis=-1, keepdims=True) * gp
    gate_in = jnp.sum(jax.nn.one_hot(ei, MOE_PER_GROUP, dtype=jnp.float32) * ew[..., None], axis=1)
    gate = (g_onehot[:, :, None] * gate_in[:, None, :]).reshape(-1, MOE_EXPERTS)
    h = jax.nn.silu(jnp.einsum('nd,edf->nef', xt, w_gate)) * jnp.einsum('nd,edf->nef', xt, w_up)
    y = jnp.einsum('nef,efd->nd', h * gate.astype(h.dtype)[..., None], w_down)
    return y.reshape(B, T, D)


def _trunk(x, rg_h, rg_conv, gdn_s, gdn_conv, ssd_h, ssd_conv, p):
    rg_h_new, rg_conv_new, gdn_s_new, gdn_conv_new, ssd_h_new, ssd_conv_new = [], [], [], [], [], []
    for i in range(DEPTH):
        j = i // N_MIXERS
        if i % N_MIXERS == 0:
            m, h, c = _rglru_mixer(x, rg_h[j], rg_conv[j], p['rg_w_in'][j], p['rg_conv_w'][j], p['rg_conv_b'][j],
                                   p['rg_w_rgate'][j], p['rg_b_rgate'][j], p['rg_w_igate'][j], p['rg_b_igate'][j],
                                   p['rg_lambda'][j], p['rg_w_out'][j])
            rg_h_new.append(h)
            rg_conv_new.append(c)
        elif i % N_MIXERS == 1:
            m, h, c = _gdn_mixer(x, gdn_s[j], gdn_conv[j], p['gdn_w_in'][j], p['gdn_conv_w'][j], p['gdn_a_log'][j],
                                 p['gdn_dt_bias'][j], p['gdn_norm_w'][j], p['gdn_w_out'][j])
            gdn_s_new.append(h)
            gdn_conv_new.append(c)
        else:
            m, h, c = _ssd_mixer(x, ssd_h[j], ssd_conv[j], p['ssd_w_in'][j], p['ssd_conv_w'][j], p['ssd_conv_b'][j],
                                 p['ssd_a_log'][j], p['ssd_dt_bias'][j], p['ssd_d'][j], p['ssd_norm_w'][j],
                                 p['ssd_w_out'][j])
            ssd_h_new.append(h)
            ssd_conv_new.append(c)
        x = _layer_norm(DN_ALPHA * x + m, p['ln_mix_g'][i], p['ln_mix_b'][i])
        f = _moe(x, p['moe_router_group_w'][i], p['moe_router_group_b'][i], p['moe_router_expert_w'][i],
                 p['moe_router_expert_b'][i], p['moe_w_gate'][i], p['moe_w_up'][i], p['moe_w_down'][i])
        x = _layer_norm(DN_ALPHA * x + f, p['ln_ffn_g'][i], p['ln_ffn_b'][i])
    return (x, jnp.stack(rg_h_new), jnp.stack(rg_conv_new), jnp.stack(gdn_s_new), jnp.stack(gdn_conv_new),
            jnp.stack(ssd_h_new), jnp.stack(ssd_conv_new))


def setup_inputs(seed: int = 0) -> dict:
    key = jax.random.key(seed)
    ks = iter(jax.random.split(key, 64))

    def nrm(shape, scale):
        return scale * jax.random.normal(next(ks), shape, jnp.float32)

    def unif(shape, lo, hi):
        return jax.random.uniform(next(ks), shape, jnp.float32, lo, hi)

    def dt_bias(shape):
        dt = jnp.exp(unif(shape, math.log(1e-3), math.log(1e-1)))
        return dt + jnp.log(-jnp.expm1(-dt))

    def gain(shape):
        return 1.0 + nrm(shape, 0.02)

    d = D_MODEL ** -0.5
    a_rg = unif((N_RG_LAYERS, D_RNN), 0.9, 0.999) ** (1.0 / RG_C)
    return {
        'x_prompt': nrm((BATCH, SEQ, D_MODEL), 1.0),
        'x_sample': nrm((DEC_BATCH, DEC_SEQ, D_MODEL), 1.0),
        'state_rg_h': nrm((N_RG_LAYERS, DEC_BATCH, D_RNN), 0.5),
        'state_rg_conv': nrm((N_RG_LAYERS, DEC_BATCH, CONV_K - 1, D_RNN), 1.0),
        'state_gdn_s': nrm((N_GDN_LAYERS, DEC_BATCH, GDN_V_HEADS, GDN_HEAD_K, GDN_HEAD_V), 0.05),
        'state_gdn_conv': nrm((N_GDN_LAYERS, DEC_BATCH, CONV_K - 1, GDN_CONV_DIM), 1.0),
        'state_ssd_h': nrm((N_SSD_LAYERS, DEC_BATCH, SSD_HEADS, SSD_HEAD_DIM, SSD_STATE), 0.1),
        'state_ssd_conv': nrm((N_SSD_LAYERS, DEC_BATCH, CONV_K - 1, SSD_CONV_DIM), 1.0),
        'ln_mix_g': gain((DEPTH, D_MODEL)),
        'ln_mix_b': nrm((DEPTH, D_MODEL), 0.02),
        'ln_ffn_g': gain((DEPTH, D_MODEL)),
        'ln_ffn_b': nrm((DEPTH, D_MODEL), 0.02),
        'moe_router_group_w': nrm((DEPTH, D_MODEL, MOE_GROUPS), d),
        'moe_router_group_b': nrm((DEPTH, MOE_GROUPS), 0.01),
        'moe_router_expert_w': nrm((DEPTH, D_MODEL, MOE_EXPERTS), d),
        'moe_router_expert_b': nrm((DEPTH, MOE_EXPERTS), 0.01),
        'moe_w_gate': nrm((DEPTH, MOE_EXPERTS, D_MODEL, MOE_D_FF), d),
        'moe_w_up': nrm((DEPTH, MOE_EXPERTS, D_MODEL, MOE_D_FF), d),
        'moe_w_down': nrm((DEPTH, MOE_EXPERTS, MOE_D_FF, D_MODEL), DN_BETA * MOE_D_FF ** -0.5),
        'rg_w_in': nrm((N_RG_LAYERS, D_MODEL, 2 * D_RNN), d),
        'rg_conv_w': nrm((N_RG_LAYERS, CONV_K, D_RNN), CONV_K ** -0.5),
        'rg_conv_b': nrm((N_RG_LAYERS, D_RNN), 0.02),
        'rg_w_rgate': nrm((N_RG_LAYERS, RG_BLOCKS, RG_BLOCK, RG_BLOCK), RG_BLOCK ** -0.5),
        'rg_b_rgate': nrm((N_RG_LAYERS, D_RNN), 0.02),
        'rg_w_igate': nrm((N_RG_LAYERS, RG_BLOCKS, RG_BLOCK, RG_BLOCK), RG_BLOCK ** -0.5),
        'rg_b_igate': nrm((N_RG_LAYERS, D_RNN), 0.02),
        'rg_lambda': jnp.log(a_rg) - jnp.log1p(-a_rg),
        'rg_w_out': nrm((N_RG_LAYERS, D_RNN, D_MODEL), DN_BETA * D_RNN ** -0.5),
        'gdn_w_in': nrm((N_GDN_LAYERS, D_MODEL, GDN_IN), d),
        'gdn_conv_w': nrm((N_GDN_LAYERS, CONV_K, GDN_CONV_DIM), CONV_K ** -0.5),
        'gdn_a_log': jnp.log(unif((N_GDN_LAYERS, GDN_V_HEADS), 1.0, 16.0)),
        'gdn_dt_bias': dt_bias((N_GDN_LAYERS, GDN_V_HEADS)),
        'gdn_norm_w': gain((N_GDN_LAYERS, GDN_HEAD_V)),
        'gdn_w_out': nrm((N_GDN_LAYERS, GDN_V_DIM, D_MODEL), DN_BETA * GDN_V_DIM ** -0.5),
        'ssd_w_in': nrm((N_SSD_LAYERS, D_MODEL, SSD_IN), d),
        'ssd_conv_w': nrm((N_SSD_LAYERS, CONV_K, SSD_CONV_DIM), CONV_K ** -0.5),
        'ssd_conv_b': nrm((N_SSD_LAYERS, SSD_CONV_DIM), 0.02),
        'ssd_a_log': jnp.log(unif((N_SSD_LAYERS, SSD_HEADS), 1.0, 16.0)),
        'ssd_dt_bias': dt_bias((N_SSD_LAYERS, SSD_HEADS)),
        'ssd_d': gain((N_SSD_LAYERS, SSD_HEADS)),
        'ssd_norm_w': gain((N_SSD_LAYERS, SSD_D_INNER)),
        'ssd_w_out': nrm((N_SSD_LAYERS, SSD_D_INNER, D_MODEL), DN_BETA * SSD_D_INNER ** -0.5),
    }


def reference(x_prompt, x_sample, state_rg_h, state_rg_conv, state_gdn_s, state_gdn_conv, state_ssd_h,
              state_ssd_conv, ln_mix_g, ln_mix_b, ln_ffn_g, ln_ffn_b, moe_router_group_w, moe_router_group_b,
              moe_router_expert_w, moe_router_expert_b, moe_w_gate, moe_w_up, moe_w_down, rg_w_in, rg_conv_w,
              rg_conv_b, rg_w_rgate, rg_b_rgate, rg_w_igate, rg_b_igate, rg_lambda, rg_w_out, gdn_w_in, gdn_conv_w,
              gdn_a_log, gdn_dt_bias, gdn_norm_w, gdn_w_out, ssd_w_in, ssd_conv_w, ssd_conv_b, ssd_a_log,
              ssd_dt_bias, ssd_d, ssd_norm_w, ssd_w_out):
    p = {
        'ln_mix_g': ln_mix_g, 'ln_mix_b': ln_mix_b, 'ln_ffn_g': ln_ffn_g, 'ln_ffn_b': ln_ffn_b,
        'moe_router_group_w': moe_router_group_w, 'moe_router_group_b': moe_router_group_b,
        'moe_router_expert_w': moe_router_expert_w, 'moe_router_expert_b': moe_router_expert_b,
        'moe_w_gate': moe_w_gate, 'moe_w_up': moe_w_up, 'moe_w_down': moe_w_down,
        'rg_w_in': rg_w_in, 'rg_conv_w': rg_conv_w, 'rg_conv_b': rg_conv_b, 'rg_w_rgate': rg_w_rgate,
        'rg_b_rgate': rg_b_rgate, 'rg_w_igate': rg_w_igate, 'rg_b_igate': rg_b_igate, 'rg_lambda': rg_lambda,
        'rg_w_out': rg_w_out,
        'gdn_w_in': gdn_w_in, 'gdn_conv_w': gdn_conv_w, 'gdn_a_log': gdn_a_log, 'gdn_dt_bias': gdn_dt_bias,
        'gdn_norm_w': gdn_norm_w, 'gdn_w_out': gdn_w_out,
        'ssd_w_in': ssd_w_in, 'ssd_conv_w': ssd_conv_w, 'ssd_conv_b': ssd_conv_b, 'ssd_a_log': ssd_a_log,
        'ssd_dt_bias': ssd_dt_bias, 'ssd_d': ssd_d, 'ssd_norm_w': ssd_norm_w, 'ssd_w_out': ssd_w_out,
    }
    nb = x_prompt.shape[0]

    def fresh(s):
        return jnp.zeros(s.shape[:1] + (nb,) + s.shape[2:], x_prompt.dtype)

    (y_prompt, rg_h_p, rg_conv_p, gdn_s_p, gdn_conv_p, ssd_h_p, ssd_conv_p) = _trunk(
        x_prompt, fresh(state_rg_h), fresh(state_rg_conv), fresh(state_gdn_s), fresh(state_gdn_conv),
        fresh(state_ssd_h), fresh(state_ssd_conv), p)
    (y_sample, rg_h_s, rg_conv_s, gdn_s_s, gdn_conv_s, ssd_h_s, ssd_conv_s) = _trunk(
        x_sample, state_rg_h, state_rg_conv, state_gdn_s, state_gdn_conv, state_ssd_h, state_ssd_conv, p)
    return (y_prompt, y_sample, rg_h_p, rg_h_s, rg_conv_p, rg_conv_s, gdn_s_p, gdn_s_s, gdn_conv_p, gdn_conv_s,
            ssd_h_p, ssd_h_s, ssd_conv_p, ssd_conv_s)
```

```python
import functools
import math

import jax
import jax.numpy as jnp
from jax import lax
from jax.experimental import pallas as pl
from jax.experimental.pallas import tpu as pltpu

F32 = jnp.float32
BF16 = jnp.bfloat16
I32 = jnp.int32

D_MODEL = 2048
DEPTH = 4
CONV_K = 4
RG_BLOCK = 256
D_RNN = 2560
RG_BLOCKS = D_RNN // RG_BLOCK
RG_C = 8.0
DN_ALPHA = (2.0 * DEPTH) ** 0.25
LN_EPS = 1e-5
RMS_EPS = 1e-6

V7X_VMEM_BYTES = 64 * 1024 * 1024
VMEM_LIMIT = 56 * 1024 * 1024


def _cparams(*sem):
    return pltpu.CompilerParams(dimension_semantics=sem, vmem_limit_bytes=VMEM_LIMIT)


def _proj_kernel(x_ref, w_ref, o_ref, wb_ref):
    @pl.when(pl.program_id(1) == 0)
    def _():
        wb_ref[...] = w_ref[...].astype(BF16)

    o_ref[...] = jnp.dot(x_ref[...], wb_ref[...], preferred_element_type=F32)


def _proj(xb, w, layer, n_out, *, tm, tn):
    m, k = xb.shape
    return pl.pallas_call(
        _proj_kernel,
        grid=(n_out // tn, m // tm),
        in_specs=[
            pl.BlockSpec((tm, k), lambda j, i: (i, 0)),
            pl.BlockSpec((None, k, tn), lambda j, i: (layer, 0, j)),
        ],
        out_specs=pl.BlockSpec((tm, tn), lambda j, i: (i, j)),
        out_shape=jax.ShapeDtypeStruct((m, n_out), F32),
        scratch_shapes=[pltpu.VMEM((k, tn), BF16)],
        compiler_params=_cparams("arbitrary", "arbitrary"),
        name="proj",
    )(xb, w)


def _rows3(a):
    return a.reshape(a.shape[0], 1, a.shape[1])


def _layer_norm_rows(z, g, b):
    mu = jnp.mean(z, axis=-1, keepdims=True)
    zc = z - mu
    var = jnp.mean(zc * zc, axis=-1, keepdims=True)
    return zc * lax.rsqrt(var + LN_EPS) * g + b


def _outproj_ln_kernel(y_ref, w_ref, res_ref, g_ref, b_ref, xo_ref, xbo_ref, acc_ref, *, nk):
    k = pl.program_id(1)

    @pl.when(k == 0)
    def _():
        acc_ref[...] = jnp.zeros_like(acc_ref)

    acc_ref[...] += jnp.dot(y_ref[...], w_ref[...].astype(BF16), preferred_element_type=F32)

    @pl.when(k == nk - 1)
    def _():
        out = _layer_norm_rows(DN_ALPHA * res_ref[...] + acc_ref[...], g_ref[...], b_ref[...])
        xo_ref[...] = out
        xbo_ref[...] = out.astype(BF16)


def _outproj_ln(yb, w, layer_w, res, g, b, layer, *, tm, tk):
    m, k = yb.shape
    nk = k // tk
    return pl.pallas_call(
        functools.partial(_outproj_ln_kernel, nk=nk),
        grid=(m // tm, nk),
        in_specs=[
            pl.BlockSpec((tm, tk), lambda i, kk: (i, kk)),
            pl.BlockSpec((None, tk, D_MODEL), lambda i, kk: (layer_w, kk, 0)),
            pl.BlockSpec((tm, D_MODEL), lambda i, kk: (i, 0)),
            pl.BlockSpec((None, 1, D_MODEL), lambda i, kk: (layer, 0, 0)),
            pl.BlockSpec((None, 1, D_MODEL), lambda i, kk: (layer, 0, 0)),
        ],
        out_specs=[
            pl.BlockSpec((tm, D_MODEL), lambda i, kk: (i, 0)),
            pl.BlockSpec((tm, D_MODEL), lambda i, kk: (i, 0)),
        ],
        out_shape=[
            jax.ShapeDtypeStruct((m, D_MODEL), F32),
            jax.ShapeDtypeStruct((m, D_MODEL), BF16),
        ],
        scratch_shapes=[pltpu.VMEM((tm, D_MODEL), F32)],
        compiler_params=_cparams("arbitrary", "arbitrary"),
        name="outproj_ln",
    )(yb, w, res, _rows3(g), _rows3(b))


def _sigmoid(x):
    return 1.0 / (1.0 + jnp.exp(-x))


def _softplus(x):
    return jnp.maximum(x, 0.0) + jnp.log1p(jnp.exp(-jnp.abs(x)))


def _silu(x):
    return x * _sigmoid(x)


def _gelu_tanh(x):
    c = math.sqrt(2.0 / math.pi)
    return 0.5 * x * (1.0 + jnp.tanh(c * (x + 0.044715 * (x * x * x))))


def _neg_expm1(x):
    t = jnp.tanh(0.5 * x)
    return -2.0 * t / (1.0 - t)


def _shift_rows(x, s, t_idx, masked):
    xs = pltpu.roll(x, s, axis=0)
    if masked:
        xs = jnp.where(t_idx >= s, xs, 0.0)
    return xs


def _rg_core_kernel(gate_ref, u_ref, cw_ref, cb_ref, wr_ref, br_ref, wi_ref, bi_ref, lam_ref,
                    yg_ref, hl_ref, *, period, hist):
    u = u_ref[...]
    rows = u.shape[0]
    t_idx = lax.broadcasted_iota(I32, (rows, 1), 0) % period
    cw = cw_ref[...]
    acc = u * cw[CONV_K - 1:CONV_K, :]
    for s in range(1, CONV_K):
        acc = acc + _shift_rows(u, s, t_idx, hist == 0) * cw[CONV_K - 1 - s:CONV_K - s, :]
    uc = acc + cb_ref[...]
    ucb = uc.astype(BF16)
    r = _sigmoid(jnp.dot(ucb, wr_ref[...].astype(BF16), preferred_element_type=F32) + br_ref[...])
    ig = _sigmoid(jnp.dot(ucb, wi_ref[...].astype(BF16), preferred_element_type=F32) + bi_ref[...])
    log_a = (-RG_C) * r * _softplus(-lam_ref[...])
    a = jnp.exp(log_a)
    b = jnp.sqrt(_neg_expm1(2.0 * log_a)) * (ig * uc)
    if hist:
        valid = t_idx >= hist
        a = jnp.where(valid, a, 1.0)
        b = jnp.where(valid, b, 0.0)
        b = jnp.where(t_idx == hist - 1, pltpu.roll(u, hist - 1, axis=0), b)
    d = 1
    while d < period:
        m = t_idx >= d
        b = jnp.where(m, a * pltpu.roll(b, d, axis=0), 0.0) + b
        a = jnp.where(m, a * pltpu.roll(a, d, axis=0), a)
        d *= 2
    h = b
    yg_ref[...] = (_gelu_tanh(gate_ref[...]) * h).astype(BF16)
    if hist:
        hl_ref[...] = h
    else:
        hl_ref[...] = h[rows - 1:rows, :]


def _rg_core(xw, nseq, period, hist, p, j):
    rows = period
    grid_b = nseq
    if hist:
        rows = nseq * period
        grid_b = 1
    c = RG_BLOCK
    vec = lambda name: pl.BlockSpec((None, 1, c), lambda b, n: (j, 0, n))
    in_specs = [
        pl.BlockSpec((rows, c), lambda b, n: (b, n)),
        pl.BlockSpec((rows, c), lambda b, n: (b, RG_BLOCKS + n)),
        pl.BlockSpec((None, CONV_K, c), lambda b, n: (j, 0, n)),
        vec("conv_b"),
        pl.BlockSpec((None, None, c, c), lambda b, n: (j, n, 0, 0)),
        vec("b_r"),
        pl.BlockSpec((None, None, c, c), lambda b, n: (j, n, 0, 0)),
        vec("b_i"),
        vec("lambda"),
    ]
    if hist:
        hl_spec = pl.BlockSpec((rows, c), lambda b, n: (b, n))
        hl_shape = jax.ShapeDtypeStruct((nseq * period, D_RNN), F32)
    else:
        hl_spec = pl.BlockSpec((None, 1, c), lambda b, n: (b, 0, n))
        hl_shape = jax.ShapeDtypeStruct((nseq, 1, D_RNN), F32)
    out_specs = [pl.BlockSpec((rows, c), lambda b, n: (b, n)), hl_spec]
    out_shape = [jax.ShapeDtypeStruct((nseq * period, D_RNN), BF16), hl_shape]
    return pl.pallas_call(
        functools.partial(_rg_core_kernel, period=period, hist=hist),
        grid=(grid_b, RG_BLOCKS),
        in_specs=in_specs,
        out_specs=out_specs,
        out_shape=out_shape,
        compiler_params=_cparams("arbitrary", "arbitrary"),
        name="rg_core",
    )(xw, xw, p["rg_conv_w"], _rows3(p["rg_conv_b"]), p["rg_w_rgate"], _rows3(p["rg_b_rgate"]),
      p["rg_w_igate"], _rows3(p["rg_b_igate"]), _rows3(p["rg_lambda"]))


def _rg_layer(xb_all, x_all, n_prompt_seq, t_prompt, n_sample_seq, t_sample, h0, buf, p, j, layer):
    n_p = n_prompt_seq * t_prompt
    xw = _proj(xb_all, p["rg_w_in"], j, 2 * D_RNN, tm=_row_tile(xb_all.shape[0]), tn=1024)
    yg_p, hl_p = _rg_core(xw[:n_p], n_prompt_seq, t_prompt, 0, p, j)
    nb_p = xw[:n_p].reshape(n_prompt_seq, t_prompt, 2 * D_RNN)[:, t_prompt - (CONV_K - 1):, D_RNN:]
    hist = CONV_K
    period = hist + t_sample
    xs = xw[n_p:].reshape(n_sample_seq, t_sample, 2 * D_RNN)
    hist_u = jnp.concatenate([h0[:, None, :], buf], axis=1)
    hist_rows = jnp.concatenate([jnp.zeros_like(hist_u), hist_u], axis=-1)
    ext = jnp.concatenate([hist_rows, xs], axis=1)
    nb_s = ext[:, period - (CONV_K - 1):, D_RNN:]
    yg_s, h_s = _rg_core(ext.reshape(n_sample_seq * period, 2 * D_RNN), n_sample_seq, period, hist, p, j)
    yg_s = yg_s.reshape(n_sample_seq, period, D_RNN)[:, hist:].reshape(n_sample_seq * t_sample, D_RNN)
    hl_s = h_s.reshape(n_sample_seq, period, D_RNN)[:, period - 1]
    yg = jnp.concatenate([yg_p, yg_s], axis=0)
    x_new, xb_new = _outproj_ln(yg, p["rg_w_out"], j, x_all, p["ln_mix_g"], p["ln_mix_b"], layer,
                                tm=_row_tile_small(yg.shape[0]), tk=512)
    return x_new, xb_new, (hl_p[:, 0, :], nb_p), (hl_s, nb_s)


def _row_tile(m):
    for t in (1088, 1024, 512, 256, 128, 64, 32, 16):
        if m % t == 0:
            return t
    raise ValueError(m)


def _row_tile_small(m):
    for t in (544, 512, 256, 128, 64, 32, 16):
        if m % t == 0:
            return t
    raise ValueError(m)


MOE_GROUPS = 8
MOE_PER_GROUP = 8
MOE_EXPERTS = MOE_GROUPS * MOE_PER_GROUP
MOE_D_FF = D_MODEL // 4
MOE_TM = 256
ROUTER_TM = 512


def _moe_num_tiles(n_tok):
    return (2 * n_tok) // MOE_TM + MOE_EXPERTS


def _first_index_of_max(v, lane):
    vmax = jnp.max(v, axis=-1, keepdims=True)
    idx = jnp.min(jnp.where(v == vmax, lane, v.shape[-1]), axis=-1, keepdims=True)
    return vmax, idx


def _router_kernel(x_ref, wg_ref, bg_ref, we_ref, be_ref, ids_ref, wts_ref, rank_ref, cnt_ref, carry_ref):
    i = pl.program_id(0)

    @pl.when(i == 0)
    def _():
        carry_ref[...] = jnp.zeros_like(carry_ref)

    x = x_ref[...]
    tm = x.shape[0]
    glog = jnp.dot(x, wg_ref[...].astype(BF16), preferred_element_type=F32) + bg_ref[...]
    elog = jnp.dot(x, we_ref[...].astype(BF16), preferred_element_type=F32) + be_ref[...]
    glane = lax.broadcasted_iota(I32, glog.shape, 1)
    gmax, gi = _first_index_of_max(glog, glane)
    gp = 1.0 / jnp.sum(jnp.exp(glog - gmax), axis=-1, keepdims=True)
    lane = lax.broadcasted_iota(I32, elog.shape, 1)
    in_group = (lane // MOE_PER_GROUP) == gi
    neg = jnp.float32(-jnp.inf)
    m1 = jnp.where(in_group, elog, neg)
    e1, i1 = _first_index_of_max(m1, lane)
    denom = jnp.sum(jnp.exp(m1 - e1), axis=-1, keepdims=True)
    m2 = jnp.where(lane == i1, neg, m1)
    e2, i2 = _first_index_of_max(m2, lane)
    p1 = 1.0 / denom
    p2 = jnp.exp(e2 - e1) / denom
    psum = p1 + p2
    ids_ref[...] = jnp.concatenate([i1, i2], axis=1)
    wts_ref[...] = jnp.concatenate([p1 / psum * gp, p2 / psum * gp], axis=1)

    r_i = lax.broadcasted_iota(I32, (tm, tm), 0)
    c_i = lax.broadcasted_iota(I32, (tm, tm), 1)
    before = (c_i < r_i).astype(BF16)
    oh1 = lane == i1
    oh2 = lane == i2
    carry = carry_ref[...]
    tot1 = jnp.sum(oh1.astype(F32), axis=0, keepdims=True)
    tot2 = jnp.sum(oh2.astype(F32), axis=0, keepdims=True)
    c1 = jnp.dot(before, oh1.astype(BF16), preferred_element_type=F32) + carry
    c2 = jnp.dot(before, oh2.astype(BF16), preferred_element_type=F32) + (carry + tot1)
    r1 = jnp.sum(jnp.where(oh1, c1, 0.0), axis=-1, keepdims=True)
    r2 = jnp.sum(jnp.where(oh2, c2, 0.0), axis=-1, keepdims=True)
    rank_ref[...] = jnp.concatenate([r1, r2], axis=1).astype(I32)
    carry = carry + tot1 + tot2
    carry_ref[...] = carry
    cnt_ref[...] = carry


def _moe_route(xb, p, layer):
    n = xb.shape[0]
    tm = ROUTER_TM if n % ROUTER_TM == 0 else n
    pair = lambda dt: jax.ShapeDtypeStruct((n, 2), dt)
    return pl.pallas_call(
        _router_kernel,
        grid=(n // tm,),
        in_specs=[
            pl.BlockSpec((tm, D_MODEL), lambda i: (i, 0)),
            pl.BlockSpec((None, D_MODEL, MOE_GROUPS), lambda i: (layer, 0, 0)),
            pl.BlockSpec((None, 1, MOE_GROUPS), lambda i: (layer, 0, 0)),
            pl.BlockSpec((None, D_MODEL, MOE_EXPERTS), lambda i: (layer, 0, 0)),
            pl.BlockSpec((None, 1, MOE_EXPERTS), lambda i: (layer, 0, 0)),
        ],
        out_specs=[
            pl.BlockSpec((tm, 2), lambda i: (i, 0)),
            pl.BlockSpec((tm, 2), lambda i: (i, 0)),
            pl.BlockSpec((tm, 2), lambda i: (i, 0)),
            pl.BlockSpec((1, MOE_EXPERTS), lambda i: (0, 0)),
        ],
        out_shape=[pair(I32), pair(F32), pair(I32), jax.ShapeDtypeStruct((1, MOE_EXPERTS), F32)],
        scratch_shapes=[pltpu.VMEM((1, MOE_EXPERTS), F32)],
        compiler_params=_cparams("arbitrary"),
        name="moe_router",
    )(xb, p["moe_router_group_w"], _rows3(p["moe_router_group_b"]), p["moe_router_expert_w"],
      _rows3(p["moe_router_expert_b"]))


def _moe_plan_kernel(cnt_ref, ids_ref, rank_ref, pos_ref, te_ref, nused_ref, *, n_tiles):
    cnt = cnt_ref[...]
    nt = jnp.floor((cnt + (MOE_TM - 1)) * (1.0 / MOE_TM))
    e_r = lax.broadcasted_iota(I32, (MOE_EXPERTS, MOE_EXPERTS), 0)
    e_c = lax.broadcasted_iota(I32, (MOE_EXPERTS, MOE_EXPERTS), 1)
    nt_b = nt.astype(BF16)
    first = jnp.dot(nt_b, (e_r < e_c).astype(BF16), preferred_element_type=F32)
    last = first + nt
    ids = ids_ref[...]
    lane = lax.broadcasted_iota(I32, (ids.shape[0], MOE_EXPERTS), 1)
    cols = []
    for k in range(2):
        sel = lane == ids[:, k:k + 1]
        cols.append(jnp.sum(jnp.where(sel, first, 0.0), axis=-1, keepdims=True))
    tile0 = jnp.concatenate(cols, axis=1).astype(I32)
    pos_ref[...] = tile0 * MOE_TM + rank_ref[...]
    last_col = jnp.sum(jnp.where(e_r == e_c, last, 0.0), axis=-1, keepdims=True)
    t_row = lax.broadcasted_iota(I32, (MOE_EXPERTS, n_tiles), 1).astype(F32)
    te = jnp.sum((last_col <= t_row).astype(F32), axis=0, keepdims=True)
    te_ref[...] = jnp.minimum(te, MOE_EXPERTS - 1.0).astype(I32)
    nused_ref[...] = jnp.sum(nt, axis=-1, keepdims=True).astype(I32)


def _moe_plan(cnt, ids, rank):
    n = ids.shape[0]
    tm = ROUTER_TM if n % ROUTER_TM == 0 else n
    n_tiles = _moe_num_tiles(n)
    return pl.pallas_call(
        functools.partial(_moe_plan_kernel, n_tiles=n_tiles),
        grid=(n // tm,),
        in_specs=[
            pl.BlockSpec((1, MOE_EXPERTS), lambda i: (0, 0)),
            pl.BlockSpec((tm, 2), lambda i: (i, 0)),
            pl.BlockSpec((tm, 2), lambda i: (i, 0)),
        ],
        out_specs=[
            pl.BlockSpec((tm, 2), lambda i: (i, 0)),
            pl.BlockSpec((1, n_tiles), lambda i: (0, 0)),
            pl.BlockSpec((1, 1), lambda i: (0, 0)),
        ],
        out_shape=[
            jax.ShapeDtypeStruct((n, 2), I32),
            jax.ShapeDtypeStruct((1, n_tiles), I32),
            jax.ShapeDtypeStruct((1, 1), I32),
        ],
        compiler_params=_cparams("arbitrary"),
        name="moe_plan",
    )(cnt, ids, rank)


def _row_copy(src_hbm, src_row, dst_ref, dst_row, sem):
    return pltpu.make_async_copy(src_hbm.at[pl.ds(src_row, 1)], dst_ref.at[pl.ds(dst_row, 1)], sem)


def _moe_dispatch_kernel(pos_ref, x_hbm, xs_in_hbm, xs_hbm, sem):
    del xs_in_hbm
    tm = pos_ref.shape[0] // 2
    base = pl.program_id(0) * tm

    def start(n, c):
        _row_copy(x_hbm, base + n, xs_hbm, pos_ref[2 * n], sem).start()
        _row_copy(x_hbm, base + n, xs_hbm, pos_ref[2 * n + 1], sem).start()
        return c

    lax.fori_loop(0, tm, start, 0)

    def wait(n, c):
        _row_copy(x_hbm, 0, xs_hbm, 0, sem).wait()
        return c

    lax.fori_loop(0, 2 * tm, wait, 0)


def _moe_dispatch(pos_flat, x, n_rows):
    n = x.shape[0]
    tm = ROUTER_TM if n % ROUTER_TM == 0 else n
    xs0 = jnp.zeros((n_rows, D_MODEL), F32)
    return pl.pallas_call(
        _moe_dispatch_kernel,
        grid=(n // tm,),
        in_specs=[
            pl.BlockSpec((2 * tm,), lambda i: (i,), memory_space=pltpu.SMEM),
            pl.BlockSpec(memory_space=pl.ANY),
            pl.BlockSpec(memory_space=pl.ANY),
        ],
        out_specs=pl.BlockSpec(memory_space=pl.ANY),
        out_shape=jax.ShapeDtypeStruct((n_rows, D_MODEL), F32),
        scratch_shapes=[pltpu.SemaphoreType.DMA(())],
        input_output_aliases={2: 0},
        compiler_params=_cparams("arbitrary"),
        name="moe_dispatch",
    )(pos_flat, x, xs0)


def _moe_gmm_kernel(te_ref, nused_ref, xs_ref, wg_ref, wu_ref, wd_ref, o_ref, wgb_ref, wub_ref, wdb_ref):
    i = pl.program_id(0)
    used = i < nused_ref[0]
    new_expert = jnp.logical_or(i == 0, te_ref[i] != te_ref[jnp.maximum(i - 1, 0)])

    @pl.when(jnp.logical_and(used, new_expert))
    def _():
        wgb_ref[...] = wg_ref[...].astype(BF16)
        wub_ref[...] = wu_ref[...].astype(BF16)
        wdb_ref[...] = wd_ref[...].astype(BF16)

    @pl.when(used)
    def _():
        xb = xs_ref[...].astype(BF16)
        g = jnp.dot(xb, wgb_ref[...], preferred_element_type=F32)
        u = jnp.dot(xb, wub_ref[...], preferred_element_type=F32)
        h = (_silu(g) * u).astype(BF16)
        o_ref[...] = jnp.dot(h, wdb_ref[...], preferred_element_type=F32)

    @pl.when(jnp.logical_not(used))
    def _():
        o_ref[...] = jnp.zeros_like(o_ref)


def _moe_gmm(te, nused, xs, p, layer):
    n_rows = xs.shape[0]
    n_tiles = n_rows // MOE_TM
    wspec = lambda a, b: pl.BlockSpec((None, None, a, b), lambda i, te, nu: (layer, te[i], 0, 0))
    return pl.pallas_call(
        _moe_gmm_kernel,
        grid_spec=pltpu.PrefetchScalarGridSpec(
            num_scalar_prefetch=2,
            grid=(n_tiles,),
            in_specs=[
                pl.BlockSpec((MOE_TM, D_MODEL), lambda i, te, nu: (jnp.minimum(i, nu[0] - 1), 0)),
                wspec(D_MODEL, MOE_D_FF),
                wspec(D_MODEL, MOE_D_FF),
                wspec(MOE_D_FF, D_MODEL),
            ],
            out_specs=pl.BlockSpec((MOE_TM, D_MODEL), lambda i, te, nu: (i, 0)),
            scratch_shapes=[
                pltpu.VMEM((D_MODEL, MOE_D_FF), BF16),
                pltpu.VMEM((D_MODEL, MOE_D_FF), BF16),
                pltpu.VMEM((MOE_D_FF, D_MODEL), BF16),
            ],
        ),
        out_shape=jax.ShapeDtypeStruct((n_rows, D_MODEL), F32),
        compiler_params=_cparams("arbitrary"),
        name="moe_gmm",
    )(te, nused, xs, p["moe_w_gate"], p["moe_w_up"], p["moe_w_down"])


def _moe_combine_ln_kernel(pos_ref, o_hbm, wts_ref, res_ref, g_ref, b_ref, xo_ref, xbo_ref, buf_ref, sem):
    tm = res_ref.shape[0]

    def start(n, c):
        _row_copy(o_hbm, pos_ref[2 * n], buf_ref.at[0], n, sem).start()
        _row_copy(o_hbm, pos_ref[2 * n + 1], buf_ref.at[1], n, sem).start()
        return c

    lax.fori_loop(0, tm, start, 0)

    def wait(n, c):
        _row_copy(o_hbm, 0, buf_ref.at[0], 0, sem).wait()
        return c

    lax.fori_loop(0, 2 * tm, wait, 0)
    w = wts_ref[...]
    y = w[:, 0:1] * buf_ref[0] + w[:, 1:2] * buf_ref[1]
    out = _layer_norm_rows(DN_ALPHA * res_ref[...] + y, g_ref[...], b_ref[...])
    xo_ref[...] = out
    xbo_ref[...] = out.astype(BF16)


def _moe_combine_ln(pos_flat, o, wts, res, g, b, layer):
    n = res.shape[0]
    tm = ROUTER_TM if n % ROUTER_TM == 0 else n
    return pl.pallas_call(
        _moe_combine_ln_kernel,
        grid=(n // tm,),
        in_specs=[
            pl.BlockSpec((2 * tm,), lambda i: (i,), memory_space=pltpu.SMEM),
            pl.BlockSpec(memory_space=pl.ANY),
            pl.BlockSpec((tm, 2), lambda i: (i, 0)),
            pl.BlockSpec((tm, D_MODEL), lambda i: (i, 0)),
            pl.BlockSpec((None, 1, D_MODEL), lambda i: (layer, 0, 0)),
            pl.BlockSpec((None, 1, D_MODEL), lambda i: (layer, 0, 0)),
        ],
        out_specs=[
            pl.BlockSpec((tm, D_MODEL), lambda i: (i, 0)),
            pl.BlockSpec((tm, D_MODEL), lambda i: (i, 0)),
        ],
        out_shape=[
            jax.ShapeDtypeStruct((n, D_MODEL), F32),
            jax.ShapeDtypeStruct((n, D_MODEL), BF16),
        ],
        scratch_shapes=[pltpu.VMEM((2, tm, D_MODEL), F32), pltpu.SemaphoreType.DMA(())],
        compiler_params=_cparams("arbitrary"),
        name="moe_combine_ln",
    )(pos_flat, o, wts, res, _rows3(g), _rows3(b))


def _moe_layer(x, xb, p, layer):
    n = x.shape[0]
    ids, wts, rank, cnt = _moe_route(xb, p, layer)
    pos, te, nused = _moe_plan(cnt, ids, rank)
    pos_flat = pos.reshape(2 * n)
    xs = _moe_dispatch(pos_flat, x, _moe_num_tiles(n) * MOE_TM)
    o = _moe_gmm(te.reshape(-1), nused.reshape(-1), xs, p, layer)
    return _moe_combine_ln(pos_flat, o, wts, x, p["ln_ffn_g"], p["ln_ffn_b"], layer)


GDN_QK_HEADS = 16
GDN_V_HEADS = 32
GDN_HEAD = 128
GDN_QK_DIM = GDN_QK_HEADS * GDN_HEAD
GDN_V_DIM = GDN_V_HEADS * GDN_HEAD
GDN_CONV_DIM = 2 * GDN_QK_DIM + GDN_V_DIM
GDN_CHUNK = 64
CONV_CB = 512


def _conv_act_kernel(u_ref, cw_ref, cb_ref, o_ref, *, period, masked, n_q_blocks, n_k_blocks):
    u = u_ref[...]
    rows, cblk = u.shape
    t_idx = lax.broadcasted_iota(I32, (rows, 1), 0) % period
    cw = cw_ref[...]
    acc = u * cw[CONV_K - 1:CONV_K, :]
    for s in range(1, CONV_K):
        acc = acc + _shift_rows(u, s, t_idx, masked) * cw[CONV_K - 1 - s:CONV_K - s, :]
    y = _silu(acc + cb_ref[...])
    if n_q_blocks or n_k_blocks:
        n = pl.program_id(1)
        is_q = n < n_q_blocks
        is_k = jnp.logical_and(n >= n_q_blocks, n < n_q_blocks + n_k_blocks)
        parts = []
        for hh in range(cblk // GDN_HEAD):
            yh = y[:, hh * GDN_HEAD:(hh + 1) * GDN_HEAD]
            rs = lax.rsqrt(jnp.sum(yh * yh, axis=-1, keepdims=True) + RMS_EPS)
            scale = jnp.where(is_q, rs * GDN_HEAD ** -0.5, jnp.where(is_k, rs, 1.0))
            parts.append(yh * scale)
        y = jnp.concatenate(parts, axis=1)
    o_ref[...] = y


def _conv_act(xw, nseq, period, masked, conv_w, conv_b, j, n_ch, *, n_q_blocks=0, n_k_blocks=0, col0=0):
    rows = period if masked else nseq * period
    grid_b = nseq if masked else 1
    c = CONV_CB
    cb0 = col0 // c
    return pl.pallas_call(
        functools.partial(_conv_act_kernel, period=period, masked=masked, n_q_blocks=n_q_blocks,
                          n_k_blocks=n_k_blocks),
        grid=(grid_b, n_ch // c),
        in_specs=[
            pl.BlockSpec((rows, c), lambda b, n: (b, cb0 + n)),
            pl.BlockSpec((None, CONV_K, c), lambda b, n: (j, 0, n)),
            pl.BlockSpec((None, 1, c), lambda b, n: (j, 0, n)),
        ],
        out_specs=pl.BlockSpec((rows, c), lambda b, n: (b, n)),
        out_shape=jax.ShapeDtypeStruct((nseq * period, n_ch), F32),
        compiler_params=_cparams("arbitrary", "arbitrary"),
        name="conv_act",
    )(xw, conv_w, conv_b)


def _scan_rows_add(g, t_idx, chunk):
    d = 1
    while d < chunk:
        g = g + jnp.where(t_idx >= d, pltpu.roll(g, d, axis=0), 0.0)
        d *= 2
    return g


def _gdn_gates_kernel(xb_ref, w_ref, alog_ref, dtb_ref, bg_ref, *, chunk):
    ba = jnp.dot(xb_ref[...], w_ref[...].astype(BF16), preferred_element_type=F32)
    b = ba[:, :GDN_V_HEADS]
    a = ba[:, GDN_V_HEADS:]
    beta = _sigmoid(b)
    g = -jnp.exp(alog_ref[...]) * _softplus(a + dtb_ref[...])
    t_idx = lax.broadcasted_iota(I32, (g.shape[0], 1), 0) % chunk
    bg_ref[...] = jnp.concatenate([beta, _scan_rows_add(g, t_idx, chunk)], axis=1)


def _gdn_gates(xb, row0, n_rows, w_ba, a_log, dt_bias, j, chunk):
    tm = _row_tile_small(n_rows)
    tm = tm if tm % chunk == 0 and row0 % tm == 0 else n_rows
    r0 = row0 // tm
    return pl.pallas_call(
        functools.partial(_gdn_gates_kernel, chunk=chunk),
        grid=(n_rows // tm,),
        in_specs=[
            pl.BlockSpec((tm, D_MODEL), lambda i: (r0 + i, 0)),
            pl.BlockSpec((D_MODEL, 2 * GDN_V_HEADS), lambda i: (0, 0)),
            pl.BlockSpec((None, 1, GDN_V_HEADS), lambda i: (j, 0, 0)),
            pl.BlockSpec((None, 1, GDN_V_HEADS), lambda i: (j, 0, 0)),
        ],
        out_specs=pl.BlockSpec((tm, 2 * GDN_V_HEADS), lambda i: (i, 0)),
        out_shape=jax.ShapeDtypeStruct((n_rows, 2 * GDN_V_HEADS), F32),
        compiler_params=_cparams("arbitrary"),
        name="gdn_gates",
    )(xb, w_ba, _rows3(a_log), _rows3(dt_bias))


def _split_bf16(a):
    hi = a.astype(BF16)
    return hi, (a - hi.astype(F32)).astype(BF16)


def _dot3(a, b):
    a_hi, a_lo = _split_bf16(a)
    b_hi, b_lo = _split_bf16(b)
    d = functools.partial(jnp.dot, preferred_element_type=F32)
    return d(a_hi, b_hi) + (d(a_hi, b_lo) + d(a_lo, b_hi))


def _unit_lower_inverse(a, eye, size):
    t = eye - a
    pw = a
    m = 2
    while m < size:
        pw = _dot3(pw, pw)
        t = t + _dot3(t, pw)
        m *= 2
    return t


def _dot_nt(a, b):
    return lax.dot_general(a, b, (((1,), (1,)), ((), ())), preferred_element_type=F32)


def _dot_tn(a, b):
    return lax.dot_general(a, b, (((0,), (0,)), ((), ())), preferred_element_type=F32)


def _gdn_chunk_kernel(*refs, L, hp, has_state):
    if has_state:
        q_ref, k_ref, v_ref, z_ref, bg_ref, nw_ref, s0_ref, yg_ref, s_ref = refs
    else:
        q_ref, k_ref, v_ref, z_ref, bg_ref, nw_ref, yg_ref, s_ref = refs
        s0_ref = None

    @pl.when(pl.program_id(2) == 0)
    def _():
        if has_state:
            s_ref[...] = s0_ref[...]
        else:
            s_ref[...] = jnp.zeros_like(s_ref)

    bg = bg_ref[...]
    lane = lax.broadcasted_iota(I32, bg.shape, 1)
    r_i = lax.broadcasted_iota(I32, (L, L), 0)
    c_i = lax.broadcasted_iota(I32, (L, L), 1)
    causal = c_i <= r_i
    strict = c_i < r_i
    diag = c_i == r_i
    eye = diag.astype(F32)
    nw = nw_ref[...]
    hq0 = pl.program_id(1) * hp
    hd = GDN_HEAD
    for jp in range(hp):
        q = q_ref[:, jp * hd:(jp + 1) * hd]
        k = k_ref[:, jp * hd:(jp + 1) * hd]
        qb = q.astype(BF16)
        kb = k.astype(BF16)
        kk = _dot_nt(kb, kb)
        qk0 = _dot_nt(qb, kb)
        for jv in range(2):
            hv = 2 * jp + jv
            h = 2 * (hq0 + jp) + jv
            beta = jnp.sum(jnp.where(lane == h, bg, 0.0), axis=-1, keepdims=True)
            gc = jnp.sum(jnp.where(lane == GDN_V_HEADS + h, bg, 0.0), axis=-1, keepdims=True)
            g_row = jnp.sum(jnp.where(diag, gc, 0.0), axis=0, keepdims=True)
            decay = jnp.where(causal, jnp.exp(jnp.where(causal, gc - g_row, 0.0)), 0.0)
            a_mat = jnp.where(strict, beta * kk * decay, 0.0)
            t_inv = _unit_lower_inverse(a_mat, eye, L)
            e_g = jnp.exp(gc)
            g_last = gc[L - 1:L, :]
            v = v_ref[:, hv * hd:(hv + 1) * hd]
            wu = _dot3(t_inv, jnp.concatenate([k * (beta * e_g), v * beta], axis=1))
            w = wu[:, :hd]
            u0 = wu[:, hd:]
            qk = qk0 * decay
            q_dec = q * e_g
            k_dec = k * jnp.exp(g_last - gc)
            s = s_ref[hv]
            sb = s.astype(BF16)
            u = u0 - jnp.dot(w.astype(BF16), sb, preferred_element_type=F32)
            ub = u.astype(BF16)
            o = jnp.dot(q_dec.astype(BF16), sb, preferred_element_type=F32) + jnp.dot(
                qk.astype(BF16), ub, preferred_element_type=F32)
            s_ref[hv] = jnp.exp(g_last) * s + _dot_tn(k_dec.astype(BF16), ub)
            on = o * lax.rsqrt(jnp.mean(o * o, axis=-1, keepdims=True) + RMS_EPS) * nw
            z = z_ref[:, hv * hd:(hv + 1) * hd]
            yg_ref[:, hv * hd:(hv + 1) * hd] = (on * _silu(z)).astype(BF16)


def _gdn_core(qkv, xw, bg, norm_w, j, s0, nseq, n_chunks, L, hp):
    hd = GDN_HEAD
    row = lambda b, g, n: b * n_chunks + n
    in_specs = [
        pl.BlockSpec((L, hp * hd), lambda b, g, n: (row(b, g, n), g)),
        pl.BlockSpec((L, hp * hd), lambda b, g, n: (row(b, g, n), GDN_QK_DIM // (hp * hd) + g)),
        pl.BlockSpec((L, 2 * hp * hd), lambda b, g, n: (row(b, g, n), 2 * GDN_QK_DIM // (2 * hp * hd) + g)),
        pl.BlockSpec((L, 2 * hp * hd), lambda b, g, n: (row(b, g, n), GDN_CONV_DIM // (2 * hp * hd) + g)),
        pl.BlockSpec((L, 2 * GDN_V_HEADS), lambda b, g, n: (row(b, g, n), 0)),
        pl.BlockSpec((None, 1, hd), lambda b, g, n: (j, 0, 0)),
    ]
    args = [qkv, qkv, qkv, xw, bg, _rows3(norm_w)]
    s_spec = pl.BlockSpec((None, 2 * hp, hd, hd), lambda b, g, n: (b, g, 0, 0))
    if s0 is not None:
        in_specs.append(s_spec)
        args.append(s0)
    return pl.pallas_call(
        functools.partial(_gdn_chunk_kernel, L=L, hp=hp, has_state=s0 is not None),
        grid=(nseq, GDN_QK_HEADS // hp, n_chunks),
        in_specs=in_specs,
        out_specs=[pl.BlockSpec((L, 2 * hp * hd), lambda b, g, n: (row(b, g, n), g)), s_spec],
        out_shape=[
            jax.ShapeDtypeStruct((nseq * n_chunks * L, GDN_V_DIM), BF16),
            jax.ShapeDtypeStruct((nseq, GDN_V_HEADS, hd, hd), F32),
        ],
        compiler_params=_cparams("arbitrary", "arbitrary", "arbitrary"),
        name="gdn_core",
    )(*args)


def _gdn_layer(xb_all, x_all, n_prompt_seq, t_prompt, n_sample_seq, t_sample, s0, buf, p, j, layer):
    n_p = n_prompt_seq * t_prompt
    n_s = n_sample_seq * t_sample
    n_main = GDN_CONV_DIM + GDN_V_DIM
    xw = _proj(xb_all, p["gdn_w_in"], j, n_main, tm=_row_tile(xb_all.shape[0]), tn=1024)
    w_ba = p["gdn_w_in"][j, :, n_main:]
    cw, cb = p["gdn_conv_w"], jnp.zeros((p["gdn_conv_w"].shape[0], 1, GDN_CONV_DIM), F32)
    nqb = GDN_QK_DIM // CONV_CB
    qkv_p = _conv_act(xw, n_prompt_seq, t_prompt, True, cw, cb, j, GDN_CONV_DIM, n_q_blocks=nqb, n_k_blocks=nqb)
    nb_p = xw[:n_p].reshape(n_prompt_seq, t_prompt, n_main)[:, t_prompt - (CONV_K - 1):, :GDN_CONV_DIM]
    chunk = math.gcd(t_prompt, GDN_CHUNK)
    bg_p = _gdn_gates(xb_all, 0, n_p, w_ba, p["gdn_a_log"], p["gdn_dt_bias"], j, chunk)
    yg_p, s_p = _gdn_core(qkv_p, xw, bg_p, p["gdn_norm_w"], j, None, n_prompt_seq, t_prompt // chunk, chunk, 2)
    hist = CONV_K
    period = hist + t_sample
    xs = xw[n_p:].reshape(n_sample_seq, t_sample, n_main)
    hist_rows = jnp.concatenate([jnp.zeros((n_sample_seq, 1, GDN_CONV_DIM), F32), buf], axis=1)
    hist_rows = jnp.concatenate([hist_rows, jnp.zeros((n_sample_seq, hist, GDN_V_DIM), F32)], axis=-1)
    ext = jnp.concatenate([hist_rows, xs], axis=1)
    nb_s = ext[:, period - (CONV_K - 1):, :GDN_CONV_DIM]
    ext = ext.reshape(n_sample_seq * period, n_main)
    qkv_s = _conv_act(ext, n_sample_seq, period, False, cw, cb, j, GDN_CONV_DIM, n_q_blocks=nqb, n_k_blocks=nqb)
    bg_s = _gdn_gates(xb_all, n_p, n_s, w_ba, p["gdn_a_log"], p["gdn_dt_bias"], j, t_sample)
    bg_s = jnp.concatenate([jnp.zeros((n_sample_seq, hist, 2 * GDN_V_HEADS), F32),
                            bg_s.reshape(n_sample_seq, t_sample, 2 * GDN_V_HEADS)], axis=1)
    yg_s, s_s = _gdn_core(qkv_s, ext, bg_s.reshape(n_sample_seq * period, 2 * GDN_V_HEADS), p["gdn_norm_w"], j,
                          s0, n_sample_seq, 1, period, 2)
    yg_s = yg_s.reshape(n_sample_seq, period, GDN_V_DIM)[:, hist:].reshape(n_s, GDN_V_DIM)
    yg = jnp.concatenate([yg_p, yg_s], axis=0)
    x_new, xb_new = _outproj_ln(yg, p["gdn_w_out"], j, x_all, p["ln_mix_g"], p["ln_mix_b"], layer,
                                tm=_row_tile_small(yg.shape[0]), tk=512)
    return x_new, xb_new, (s_p, nb_p), (s_s, nb_s)


SSD_D_INNER = 2 * D_MODEL
SSD_HEAD_DIM = 64
SSD_HEADS = SSD_D_INNER // SSD_HEAD_DIM
SSD_GROUPS = 8
SSD_HPG = SSD_HEADS // SSD_GROUPS
SSD_STATE = 128
SSD_CONV_DIM = SSD_D_INNER + 2 * SSD_GROUPS * SSD_STATE
SSD_CHUNK = 128
SSD_GW = SSD_HPG * SSD_HEAD_DIM


def _ssd_gates_kernel(xb_ref, w_ref, alog_ref, dtb_ref, da_ref, *, chunk):
    raw = jnp.dot(xb_ref[...], w_ref[...].astype(BF16), preferred_element_type=F32)
    dt = _softplus(raw + dtb_ref[...])
    a = dt * (-jnp.exp(alog_ref[...]))
    t_idx = lax.broadcasted_iota(I32, (a.shape[0], 1), 0) % chunk
    da_ref[...] = jnp.concatenate([dt, _scan_rows_add(a, t_idx, chunk)], axis=1)


def _ssd_gates(xb, row0, n_rows, w_dt, a_log, dt_bias, j, chunk):
    tm = _row_tile_small(n_rows)
    tm = tm if tm % chunk == 0 and row0 % tm == 0 else n_rows
    r0 = row0 // tm
    return pl.pallas_call(
        functools.partial(_ssd_gates_kernel, chunk=chunk),
        grid=(n_rows // tm,),
        in_specs=[
            pl.BlockSpec((tm, D_MODEL), lambda i: (r0 + i, 0)),
            pl.BlockSpec((D_MODEL, SSD_HEADS), lambda i: (0, 0)),
            pl.BlockSpec((None, 1, SSD_HEADS), lambda i: (j, 0, 0)),
            pl.BlockSpec((None, 1, SSD_HEADS), lambda i: (j, 0, 0)),
        ],
        out_specs=pl.BlockSpec((tm, 2 * SSD_HEADS), lambda i: (i, 0)),
        out_shape=jax.ShapeDtypeStruct((n_rows, 2 * SSD_HEADS), F32),
        compiler_params=_cparams("arbitrary"),
        name="ssd_gates",
    )(xb, w_dt, _rows3(a_log), _rows3(dt_bias))


def _ssd_chunk_kernel(*refs, L, has_state):
    if has_state:
        x_ref, b_ref, c_ref, z_ref, da_ref, dsk_ref, nw_ref, h0_ref, yg_ref, h_ref = refs
    else:
        x_ref, b_ref, c_ref, z_ref, da_ref, dsk_ref, nw_ref, yg_ref, h_ref = refs
        h0_ref = None

    @pl.when(pl.program_id(2) == 0)
    def _():
        if has_state:
            h_ref[...] = h0_ref[...]
        else:
            h_ref[...] = jnp.zeros_like(h_ref)

    g = pl.program_id(1)
    da = da_ref[...]
    da_lane = lax.broadcasted_iota(I32, da.shape, 1)
    dsk = dsk_ref[...]
    dsk_lane = lax.broadcasted_iota(I32, dsk.shape, 1)
    r_i = lax.broadcasted_iota(I32, (L, L), 0)
    c_i = lax.broadcasted_iota(I32, (L, L), 1)
    causal = c_i <= r_i
    diag = c_i == r_i
    pd = 2 * SSD_HEAD_DIM
    first_half = lax.broadcasted_iota(I32, (1, pd), 1) < SSD_HEAD_DIM
    top_half = lax.broadcasted_iota(I32, (pd, 1), 0) < SSD_HEAD_DIM
    bmb = b_ref[...].astype(BF16)
    cmb = c_ref[...].astype(BF16)
    cb = _dot_nt(cmb, bmb)
    ys = []
    for pp in range(SSD_HPG // 2):
        x = x_ref[:, pp * pd:(pp + 1) * pd]
        per_head = []
        for jv in range(2):
            head = g * SSD_HPG + 2 * pp + jv
            dt = jnp.sum(jnp.where(da_lane == head, da, 0.0), axis=-1, keepdims=True)
            ac = jnp.sum(jnp.where(da_lane == SSD_HEADS + head, da, 0.0), axis=-1, keepdims=True)
            ac_row = jnp.sum(jnp.where(diag, ac, 0.0), axis=0, keepdims=True)
            decay = jnp.where(causal, jnp.exp(jnp.where(causal, ac - ac_row, 0.0)), 0.0)
            ac_last = ac[L - 1:L, :]
            dskip = jnp.sum(jnp.where(dsk_lane == head, dsk, 0.0), axis=-1, keepdims=True)
            per_head.append((dt, ac, (cb * decay).astype(BF16), jnp.exp(ac_last - ac), jnp.exp(ac_last), dskip))
        (dt_e, ac_e, m_e, te_e, cd_e, ds_e), (dt_o, ac_o, m_o, te_o, cd_o, ds_o) = per_head
        xdt = x * jnp.where(first_half, dt_e, dt_o)
        y_diag = jnp.dot(m_e, jnp.where(first_half, xdt, 0.0).astype(BF16), preferred_element_type=F32) + jnp.dot(
            m_o, jnp.where(first_half, 0.0, xdt).astype(BF16), preferred_element_type=F32)
        states = _dot_tn((xdt * jnp.where(first_half, te_e, te_o)).astype(BF16), bmb)
        h_pair = jnp.concatenate([h_ref[2 * pp], h_ref[2 * pp + 1]], axis=0)
        y_off = _dot_nt(cmb, h_pair.astype(BF16)) * jnp.where(first_half, jnp.exp(ac_e), jnp.exp(ac_o))
        h_new = jnp.where(top_half, cd_e, cd_o) * h_pair + states
        h_ref[2 * pp] = h_new[:SSD_HEAD_DIM]
        h_ref[2 * pp + 1] = h_new[SSD_HEAD_DIM:]
        ys.append(y_diag + y_off + jnp.where(first_half, ds_e, ds_o) * x)
    y = jnp.concatenate(ys, axis=1) * _silu(z_ref[...])
    yn = y * lax.rsqrt(jnp.mean(y * y, axis=-1, keepdims=True) + RMS_EPS) * nw_ref[...]
    yg_ref[...] = yn.astype(BF16)


def _ssd_core(xbc, xw, da, d_skip, norm_w, j, h0, nseq, n_chunks, L):
    gw, st = SSD_GW, SSD_STATE
    row = lambda b, g, n: b * n_chunks + n
    in_specs = [
        pl.BlockSpec((L, gw), lambda b, g, n: (row(b, g, n), g)),
        pl.BlockSpec((L, st), lambda b, g, n: (row(b, g, n), SSD_D_INNER // st + g)),
        pl.BlockSpec((L, st), lambda b, g, n: (row(b, g, n), SSD_D_INNER // st + SSD_GROUPS + g)),
        pl.BlockSpec((L, gw), lambda b, g, n: (row(b, g, n), g)),
        pl.BlockSpec((L, 2 * SSD_HEADS), lambda b, g, n: (row(b, g, n), 0)),
        pl.BlockSpec((None, 1, SSD_HEADS), lambda b, g, n: (j, 0, 0)),
        pl.BlockSpec((None, 1, gw), lambda b, g, n: (j, 0, g)),
    ]
    args = [xbc, xbc, xbc, xw, da, _rows3(d_skip), _rows3(norm_w)]
    h_spec = pl.BlockSpec((None, SSD_HPG, SSD_HEAD_DIM, st), lambda b, g, n: (b, g, 0, 0))
    if h0 is not None:
        in_specs.append(h_spec)
        args.append(h0)
    return pl.pallas_call(
        functools.partial(_ssd_chunk_kernel, L=L, has_state=h0 is not None),
        grid=(nseq, SSD_GROUPS, n_chunks),
        in_specs=in_specs,
        out_specs=[pl.BlockSpec((L, gw), lambda b, g, n: (row(b, g, n), g)), h_spec],
        out_shape=[
            jax.ShapeDtypeStruct((nseq * n_chunks * L, SSD_D_INNER), BF16),
            jax.ShapeDtypeStruct((nseq, SSD_HEADS, SSD_HEAD_DIM, st), F32),
        ],
        compiler_params=_cparams("arbitrary", "arbitrary", "arbitrary"),
        name="ssd_core",
    )(*args)


def _ssd_layer(xb_all, x_all, n_prompt_seq, t_prompt, n_sample_seq, t_sample, h0, buf, p, j, layer):
    n_p = n_prompt_seq * t_prompt
    n_s = n_sample_seq * t_sample
    n_main = SSD_D_INNER + SSD_CONV_DIM
    xw = _proj(xb_all, p["ssd_w_in"], j, n_main, tm=_row_tile(xb_all.shape[0]), tn=1024)
    w_dt = p["ssd_w_in"][j, :, n_main:]
    cw, cb = p["ssd_conv_w"], _rows3(p["ssd_conv_b"])
    xbc_p = _conv_act(xw, n_prompt_seq, t_prompt, True, cw, cb, j, SSD_CONV_DIM, col0=SSD_D_INNER)
    nb_p = xw[:n_p].reshape(n_prompt_seq, t_prompt, n_main)[:, t_prompt - (CONV_K - 1):, SSD_D_INNER:]
    chunk = math.gcd(t_prompt, SSD_CHUNK)
    da_p = _ssd_gates(xb_all, 0, n_p, w_dt, p["ssd_a_log"], p["ssd_dt_bias"], j, chunk)
    yg_p, h_p = _ssd_core(xbc_p, xw, da_p, p["ssd_d"], p["ssd_norm_w"], j, None, n_prompt_seq, t_prompt // chunk,
                          chunk)
    hist = CONV_K
    period = hist + t_sample
    xs = xw[n_p:].reshape(n_sample_seq, t_sample, n_main)
    hist_rows = jnp.concatenate([jnp.zeros((n_sample_seq, 1, SSD_CONV_DIM), F32), buf], axis=1)
    hist_rows = jnp.concatenate([jnp.zeros((n_sample_seq, hist, SSD_D_INNER), F32), hist_rows], axis=-1)
    ext = jnp.concatenate([hist_rows, xs], axis=1)
    nb_s = ext[:, period - (CONV_K - 1):, SSD_D_INNER:]
    ext = ext.reshape(n_sample_seq * period, n_main)
    xbc_s = _conv_act(ext, n_sample_seq, period, False, cw, cb, j, SSD_CONV_DIM, col0=SSD_D_INNER)
    da_s = _ssd_gates(xb_all, n_p, n_s, w_dt, p["ssd_a_log"], p["ssd_dt_bias"], j, t_sample)
    da_s = jnp.concatenate([jnp.zeros((n_sample_seq, hist, 2 * SSD_HEADS), F32),
                            da_s.reshape(n_sample_seq, t_sample, 2 * SSD_HEADS)], axis=1)
    yg_s, h_s = _ssd_core(xbc_s, ext, da_s.reshape(n_sample_seq * period, 2 * SSD_HEADS), p["ssd_d"],
                          p["ssd_norm_w"], j, h0, n_sample_seq, 1, period)
    yg_s = yg_s.reshape(n_sample_seq, period, SSD_D_INNER)[:, hist:].reshape(n_s, SSD_D_INNER)
    yg = jnp.concatenate([yg_p, yg_s], axis=0)
    x_new, xb_new = _outproj_ln(yg, p["ssd_w_out"], j, x_all, p["ln_mix_g"], p["ln_mix_b"], layer,
                                tm=_row_tile_small(yg.shape[0]), tk=512)
    return x_new, xb_new, (h_p, nb_p), (h_s, nb_s)


N_MIXERS = 3


def kernel(x_prompt, x_sample, state_rg_h, state_rg_conv, state_gdn_s, state_gdn_conv, state_ssd_h,
           state_ssd_conv, ln_mix_g, ln_mix_b, ln_ffn_g, ln_ffn_b, moe_router_group_w, moe_router_group_b,
           moe_router_expert_w, moe_router_expert_b, moe_w_gate, moe_w_up, moe_w_down, rg_w_in, rg_conv_w,
           rg_conv_b, rg_w_rgate, rg_b_rgate, rg_w_igate, rg_b_igate, rg_lambda, rg_w_out, gdn_w_in, gdn_conv_w,
           gdn_a_log, gdn_dt_bias, gdn_norm_w, gdn_w_out, ssd_w_in, ssd_conv_w, ssd_conv_b, ssd_a_log,
           ssd_dt_bias, ssd_d, ssd_norm_w, ssd_w_out):
    p = dict(
        ln_mix_g=ln_mix_g, ln_mix_b=ln_mix_b, ln_ffn_g=ln_ffn_g, ln_ffn_b=ln_ffn_b,
        moe_router_group_w=moe_router_group_w, moe_router_group_b=moe_router_group_b,
        moe_router_expert_w=moe_router_expert_w, moe_router_expert_b=moe_router_expert_b,
        moe_w_gate=moe_w_gate, moe_w_up=moe_w_up, moe_w_down=moe_w_down,
        rg_w_in=rg_w_in, rg_conv_w=rg_conv_w, rg_conv_b=rg_conv_b, rg_w_rgate=rg_w_rgate,
        rg_b_rgate=rg_b_rgate, rg_w_igate=rg_w_igate, rg_b_igate=rg_b_igate, rg_lambda=rg_lambda,
        rg_w_out=rg_w_out,
        gdn_w_in=gdn_w_in, gdn_conv_w=gdn_conv_w, gdn_a_log=gdn_a_log, gdn_dt_bias=gdn_dt_bias,
        gdn_norm_w=gdn_norm_w, gdn_w_out=gdn_w_out,
        ssd_w_in=ssd_w_in, ssd_conv_w=ssd_conv_w, ssd_conv_b=ssd_conv_b, ssd_a_log=ssd_a_log,
        ssd_dt_bias=ssd_dt_bias, ssd_d=ssd_d, ssd_norm_w=ssd_norm_w, ssd_w_out=ssd_w_out,
    )
    bp, tp, _ = x_prompt.shape
    bs, ts, _ = x_sample.shape
    n_p = bp * tp
    x = jnp.concatenate([x_prompt.reshape(n_p, D_MODEL), x_sample.reshape(bs * ts, D_MODEL)], axis=0)
    xb = x.astype(BF16)
    layers = {0: (_rg_layer, state_rg_h, state_rg_conv), 1: (_gdn_layer, state_gdn_s, state_gdn_conv),
              2: (_ssd_layer, state_ssd_h, state_ssd_conv)}
    new_p = {0: ([], []), 1: ([], []), 2: ([], [])}
    new_s = {0: ([], []), 1: ([], []), 2: ([], [])}
    for i in range(DEPTH):
        kind, j = i % N_MIXERS, i // N_MIXERS
        fn, st, cv = layers[kind]
        x, xb, (st_p, cv_p), (st_s, cv_s) = fn(xb, x, bp, tp, bs, ts, st[j], cv[j], p, j, i)
        new_p[kind][0].append(st_p)
        new_p[kind][1].append(cv_p)
        new_s[kind][0].append(st_s)
        new_s[kind][1].append(cv_s)
        x, xb = _moe_layer(x, xb, p, i)
    outs = [x[:n_p].reshape(bp, tp, D_MODEL), x[n_p:].reshape(bs, ts, D_MODEL)]
    for kind in range(N_MIXERS):
        for which in range(2):
            outs.append(jnp.stack(new_p[kind][which]))
            outs.append(jnp.stack(new_s[kind][which]))
    return tuple(outs)
```

```python
import functools
import math

import jax
import jax.numpy as jnp
from jax import lax
from jax.experimental import pallas as pl
from jax.experimental.pallas import tpu as pltpu

F32 = jnp.float32
BF16 = jnp.bfloat16
I32 = jnp.int32

D_MODEL = 2048
DEPTH = 4
CONV_K = 4
RG_BLOCK = 256
D_RNN = 2560
RG_BLOCKS = D_RNN // RG_BLOCK
RG_C = 8.0
DN_ALPHA = (2.0 * DEPTH) ** 0.25
LN_EPS = 1e-5
RMS_EPS = 1e-6

V7X_VMEM_BYTES = 64 * 1024 * 1024
VMEM_LIMIT = 56 * 1024 * 1024


def _cparams(*sem):
    return pltpu.CompilerParams(dimension_semantics=sem, vmem_limit_bytes=VMEM_LIMIT)


def _proj_kernel(x_ref, w_ref, o_ref, wb_ref):
    @pl.when(pl.program_id(1) == 0)
    def _():
        wb_ref[...] = w_ref[...].astype(BF16)

    o_ref[...] = jnp.dot(x_ref[...], wb_ref[...], preferred_element_type=F32)


def _proj(xb, w, layer, n_out, *, tm, tn):
    m, k = xb.shape
    return pl.pallas_call(
        _proj_kernel,
        grid=(n_out // tn, m // tm),
        in_specs=[
            pl.BlockSpec((tm, k), lambda j, i: (i, 0)),
            pl.BlockSpec((None, k, tn), lambda j, i: (layer, 0, j)),
        ],
        out_specs=pl.BlockSpec((tm, tn), lambda j, i: (i, j)),
        out_shape=jax.ShapeDtypeStruct((m, n_out), F32),
        scratch_shapes=[pltpu.VMEM((k, tn), BF16)],
        compiler_params=_cparams("arbitrary", "arbitrary"),
        name="proj",
    )(xb, w)


def _rows3(a):
    return a.reshape(a.shape[0], 1, a.shape[1])


def _tail_rows(xw, nseq, period, n_rows, c0, c1):
    return jnp.stack([xw[(b + 1) * period - n_rows:(b + 1) * period, c0:c1] for b in range(nseq)])


def _layer_norm_rows(z, g, b):
    mu = jnp.mean(z, axis=-1, keepdims=True)
    zc = z - mu
    var = jnp.mean(zc * zc, axis=-1, keepdims=True)
    return zc * lax.rsqrt(var + LN_EPS) * g + b


def _outproj_ln_kernel(y_ref, w_ref, res_ref, g_ref, b_ref, xo_ref, xbo_ref, acc_ref, *, nk):
    k = pl.program_id(1)

    @pl.when(k == 0)
    def _():
        acc_ref[...] = jnp.zeros_like(acc_ref)

    acc_ref[...] += jnp.dot(y_ref[...], w_ref[...].astype(BF16), preferred_element_type=F32)

    @pl.when(k == nk - 1)
    def _():
        out = _layer_norm_rows(DN_ALPHA * res_ref[...] + acc_ref[...], g_ref[...], b_ref[...])
        xo_ref[...] = out
        xbo_ref[...] = out.astype(BF16)


def _outproj_ln(yb, w, layer_w, res, g, b, layer, *, tm, tk):
    m, k = yb.shape
    nk = k // tk
    return pl.pallas_call(
        functools.partial(_outproj_ln_kernel, nk=nk),
        grid=(m // tm, nk),
        in_specs=[
            pl.BlockSpec((tm, tk), lambda i, kk: (i, kk)),
            pl.BlockSpec((None, tk, D_MODEL), lambda i, kk: (layer_w, kk, 0)),
            pl.BlockSpec((tm, D_MODEL), lambda i, kk: (i, 0)),
            pl.BlockSpec((None, 1, D_MODEL), lambda i, kk: (layer, 0, 0)),
            pl.BlockSpec((None, 1, D_MODEL), lambda i, kk: (layer, 0, 0)),
        ],
        out_specs=[
            pl.BlockSpec((tm, D_MODEL), lambda i, kk: (i, 0)),
            pl.BlockSpec((tm, D_MODEL), lambda i, kk: (i, 0)),
        ],
        out_shape=[
            jax.ShapeDtypeStruct((m, D_MODEL), F32),
            jax.ShapeDtypeStruct((m, D_MODEL), BF16),
        ],
        scratch_shapes=[pltpu.VMEM((tm, D_MODEL), F32)],
        compiler_params=_cparams("arbitrary", "arbitrary"),
        name="outproj_ln",
    )(yb, w, res, _rows3(g), _rows3(b))


def _sigmoid(x):
    return 1.0 / (1.0 + jnp.exp(-x))


def _softplus(x):
    return jnp.maximum(x, 0.0) + jnp.log1p(jnp.exp(-jnp.abs(x)))


def _silu(x):
    return x * _sigmoid(x)


def _gelu_tanh(x):
    c = math.sqrt(2.0 / math.pi)
    return 0.5 * x * (1.0 + jnp.tanh(c * (x + 0.044715 * (x * x * x))))


def _neg_expm1(x):
    t = jnp.tanh(0.5 * x)
    return -2.0 * t / (1.0 - t)


def _shift_rows(x, s, t_idx, masked):
    xs = pltpu.roll(x, s, axis=0)
    if masked:
        xs = jnp.where(t_idx >= s, xs, 0.0)
    return xs


def _rg_core_kernel(gate_ref, u_ref, cw_ref, cb_ref, wr_ref, br_ref, wi_ref, bi_ref, lam_ref,
                    yg_ref, hl_ref, *, period, hist):
    u = u_ref[...]
    rows = u.shape[0]
    t_idx = lax.broadcasted_iota(I32, (rows, 1), 0) % period
    cw = cw_ref[...]
    acc = u * cw[CONV_K - 1:CONV_K, :]
    for s in range(1, CONV_K):
        acc = acc + _shift_rows(u, s, t_idx, hist == 0) * cw[CONV_K - 1 - s:CONV_K - s, :]
    uc = acc + cb_ref[...]
    ucb = uc.astype(BF16)
    r = _sigmoid(jnp.dot(ucb, wr_ref[...].astype(BF16), preferred_element_type=F32) + br_ref[...])
    ig = _sigmoid(jnp.dot(ucb, wi_ref[...].astype(BF16), preferred_element_type=F32) + bi_ref[...])
    log_a = (-RG_C) * r * _softplus(-lam_ref[...])
    a = jnp.exp(log_a)
    b = jnp.sqrt(_neg_expm1(2.0 * log_a)) * (ig * uc)
    if hist:
        valid = t_idx >= hist
        a = jnp.where(valid, a, 1.0)
        b = jnp.where(valid, b, 0.0)
        b = jnp.where(t_idx == hist - 1, pltpu.roll(u, hist - 1, axis=0), b)
    d = 1
    while d < period:
        m = t_idx >= d
        b = jnp.where(m, a * pltpu.roll(b, d, axis=0), 0.0) + b
        a = jnp.where(m, a * pltpu.roll(a, d, axis=0), a)
        d *= 2
    h = b
    yg_ref[...] = (_gelu_tanh(gate_ref[...]) * h).astype(BF16)
    if hist:
        hl_ref[...] = h
    else:
        hl_ref[...] = h[rows - 1:rows, :]


def _rg_core(xw, nseq, period, hist, p, j):
    rows = period
    grid_b = nseq
    if hist:
        rows = nseq * period
        grid_b = 1
    c = RG_BLOCK
    vec = lambda name: pl.BlockSpec((None, 1, c), lambda b, n: (j, 0, n))
    in_specs = [
        pl.BlockSpec((rows, c), lambda b, n: (b, n)),
        pl.BlockSpec((rows, c), lambda b, n: (b, RG_BLOCKS + n)),
        pl.BlockSpec((None, CONV_K, c), lambda b, n: (j, 0, n)),
        vec("conv_b"),
        pl.BlockSpec((None, None, c, c), lambda b, n: (j, n, 0, 0)),
        vec("b_r"),
        pl.BlockSpec((None, None, c, c), lambda b, n: (j, n, 0, 0)),
        vec("b_i"),
        vec("lambda"),
    ]
    if hist:
        hl_spec = pl.BlockSpec((rows, c), lambda b, n: (b, n))
        hl_shape = jax.ShapeDtypeStruct((nseq * period, D_RNN), F32)
    else:
        hl_spec = pl.BlockSpec((None, 1, c), lambda b, n: (b, 0, n))
        hl_shape = jax.ShapeDtypeStruct((nseq, 1, D_RNN), F32)
    out_specs = [pl.BlockSpec((rows, c), lambda b, n: (b, n)), hl_spec]
    out_shape = [jax.ShapeDtypeStruct((nseq * period, D_RNN), BF16), hl_shape]
    return pl.pallas_call(
        functools.partial(_rg_core_kernel, period=period, hist=hist),
        grid=(grid_b, RG_BLOCKS),
        in_specs=in_specs,
        out_specs=out_specs,
        out_shape=out_shape,
        compiler_params=_cparams("arbitrary", "arbitrary"),
        name="rg_core",
    )(xw, xw, p["rg_conv_w"], _rows3(p["rg_conv_b"]), p["rg_w_rgate"], _rows3(p["rg_b_rgate"]),
      p["rg_w_igate"], _rows3(p["rg_b_igate"]), _rows3(p["rg_lambda"]))


def _rg_layer(xb_all, x_all, n_prompt_seq, t_prompt, n_sample_seq, t_sample, h0, buf, p, j, layer):
    n_p = n_prompt_seq * t_prompt
    xw = _proj(xb_all, p["rg_w_in"], j, 2 * D_RNN, tm=_row_tile(xb_all.shape[0]), tn=1024)
    yg_p, hl_p = _rg_core(xw, n_prompt_seq, t_prompt, 0, p, j)
    nb_p = _tail_rows(xw, n_prompt_seq, t_prompt, CONV_K - 1, D_RNN, 2 * D_RNN)
    hist = CONV_K
    period = hist + t_sample
    xs = xw[n_p:].reshape(n_sample_seq, t_sample, 2 * D_RNN)
    hist_u = jnp.concatenate([h0[:, None, :], buf], axis=1)
    hist_rows = jnp.concatenate([jnp.zeros_like(hist_u), hist_u], axis=-1)
    ext = jnp.concatenate([hist_rows, xs], axis=1)
    nb_s = ext[:, period - (CONV_K - 1):, D_RNN:]
    yg_s, h_s = _rg_core(ext.reshape(n_sample_seq * period, 2 * D_RNN), n_sample_seq, period, hist, p, j)
    yg_s = yg_s.reshape(n_sample_seq, period, D_RNN)[:, hist:].reshape(n_sample_seq * t_sample, D_RNN)
    hl_s = h_s.reshape(n_sample_seq, period, D_RNN)[:, period - 1]
    yg = jnp.concatenate([yg_p, yg_s], axis=0)
    x_new, xb_new = _outproj_ln(yg, p["rg_w_out"], j, x_all, p["ln_mix_g"], p["ln_mix_b"], layer,
                                tm=_row_tile_small(yg.shape[0]), tk=512)
    return x_new, xb_new, (hl_p[:, 0, :], nb_p), (hl_s, nb_s)


def _row_tile(m):
    for t in (1088, 1024, 512, 256, 128, 64, 32, 16):
        if m % t == 0:
            return t
    raise ValueError(m)


def _row_tile_small(m):
    for t in (544, 512, 256, 128, 64, 32, 16):
        if m % t == 0:
            return t
    raise ValueError(m)


MOE_GROUPS = 8
MOE_PER_GROUP = 8
MOE_EXPERTS = MOE_GROUPS * MOE_PER_GROUP
MOE_D_FF = D_MODEL // 4
MOE_TM = 256
ROUTER_TM = 512


def _moe_num_tiles(n_tok):
    return (2 * n_tok) // MOE_TM + MOE_EXPERTS


def _first_index_of_max(v, lane):
    vmax = jnp.max(v, axis=-1, keepdims=True)
    idx = jnp.min(jnp.where(v == vmax, lane, v.shape[-1]), axis=-1, keepdims=True)
    return vmax, idx


def _router_kernel(x_ref, wg_ref, bg_ref, we_ref, be_ref, ids_ref, wts_ref, rank_ref, cnt_ref, carry_ref):
    i = pl.program_id(0)

    @pl.when(i == 0)
    def _():
        carry_ref[...] = jnp.zeros_like(carry_ref)

    x = x_ref[...]
    tm = x.shape[0]
    glog = jnp.dot(x, wg_ref[...].astype(BF16), preferred_element_type=F32) + bg_ref[...]
    elog = jnp.dot(x, we_ref[...].astype(BF16), preferred_element_type=F32) + be_ref[...]
    glane = lax.broadcasted_iota(I32, glog.shape, 1)
    gmax, gi = _first_index_of_max(glog, glane)
    gp = 1.0 / jnp.sum(jnp.exp(glog - gmax), axis=-1, keepdims=True)
    lane = lax.broadcasted_iota(I32, elog.shape, 1)
    in_group = (lane // MOE_PER_GROUP) == gi
    neg = jnp.float32(-jnp.inf)
    m1 = jnp.where(in_group, elog, neg)
    e1, i1 = _first_index_of_max(m1, lane)
    denom = jnp.sum(jnp.exp(m1 - e1), axis=-1, keepdims=True)
    m2 = jnp.where(lane == i1, neg, m1)
    e2, i2 = _first_index_of_max(m2, lane)
    p1 = 1.0 / denom
    p2 = jnp.exp(e2 - e1) / denom
    psum = p1 + p2
    ids_ref[...] = jnp.concatenate([i1, i2], axis=1)
    wts_ref[...] = jnp.concatenate([p1 / psum * gp, p2 / psum * gp], axis=1)

    r_i = lax.broadcasted_iota(I32, (tm, tm), 0)
    c_i = lax.broadcasted_iota(I32, (tm, tm), 1)
    before = (c_i < r_i).astype(BF16)
    oh1 = lane == i1
    oh2 = lane == i2
    carry = carry_ref[...]
    tot1 = jnp.sum(oh1.astype(F32), axis=0, keepdims=True)
    tot2 = jnp.sum(oh2.astype(F32), axis=0, keepdims=True)
    c1 = jnp.dot(before, oh1.astype(BF16), preferred_element_type=F32) + carry
    c2 = jnp.dot(before, oh2.astype(BF16), preferred_element_type=F32) + (carry + tot1)
    r1 = jnp.sum(jnp.where(oh1, c1, 0.0), axis=-1, keepdims=True)
    r2 = jnp.sum(jnp.where(oh2, c2, 0.0), axis=-1, keepdims=True)
    rank_ref[...] = jnp.concatenate([r1, r2], axis=1).astype(I32)
    carry = carry + tot1 + tot2
    carry_ref[...] = carry
    cnt_ref[...] = carry


def _moe_route(xb, p, layer):
    n = xb.shape[0]
    tm = ROUTER_TM if n % ROUTER_TM == 0 else n
    pair = lambda dt: jax.ShapeDtypeStruct((n, 2), dt)
    return pl.pallas_call(
        _router_kernel,
        grid=(n // tm,),
        in_specs=[
            pl.BlockSpec((tm, D_MODEL), lambda i: (i, 0)),
            pl.BlockSpec((None, D_MODEL, MOE_GROUPS), lambda i: (layer, 0, 0)),
            pl.BlockSpec((None, 1, MOE_GROUPS), lambda i: (layer, 0, 0)),
            pl.BlockSpec((None, D_MODEL, MOE_EXPERTS), lambda i: (layer, 0, 0)),
            pl.BlockSpec((None, 1, MOE_EXPERTS), lambda i: (layer, 0, 0)),
        ],
        out_specs=[
            pl.BlockSpec((tm, 2), lambda i: (i, 0)),
            pl.BlockSpec((tm, 2), lambda i: (i, 0)),
            pl.BlockSpec((tm, 2), lambda i: (i, 0)),
            pl.BlockSpec((1, MOE_EXPERTS), lambda i: (0, 0)),
        ],
        out_shape=[pair(I32), pair(F32), pair(I32), jax.ShapeDtypeStruct((1, MOE_EXPERTS), F32)],
        scratch_shapes=[pltpu.VMEM((1, MOE_EXPERTS), F32)],
        compiler_params=_cparams("arbitrary"),
        name="moe_router",
    )(xb, p["moe_router_group_w"], _rows3(p["moe_router_group_b"]), p["moe_router_expert_w"],
      _rows3(p["moe_router_expert_b"]))


def _moe_plan_kernel(cnt_ref, ids_ref, rank_ref, pos_ref, te_ref, nused_ref, *, n_tiles):
    cnt = cnt_ref[...]
    nt = jnp.floor((cnt + (MOE_TM - 1)) * (1.0 / MOE_TM))
    e_r = lax.broadcasted_iota(I32, (MOE_EXPERTS, MOE_EXPERTS), 0)
    e_c = lax.broadcasted_iota(I32, (MOE_EXPERTS, MOE_EXPERTS), 1)
    nt_b = nt.astype(BF16)
    first = jnp.dot(nt_b, (e_r < e_c).astype(BF16), preferred_element_type=F32)
    last = first + nt
    ids = ids_ref[...]
    lane = lax.broadcasted_iota(I32, (ids.shape[0], MOE_EXPERTS), 1)
    cols = []
    for k in range(2):
        sel = lane == ids[:, k:k + 1]
        cols.append(jnp.sum(jnp.where(sel, first, 0.0), axis=-1, keepdims=True))
    tile0 = jnp.concatenate(cols, axis=1).astype(I32)
    pos_ref[...] = tile0 * MOE_TM + rank_ref[...]
    last_col = jnp.sum(jnp.where(e_r == e_c, last, 0.0), axis=-1, keepdims=True)
    t_row = lax.broadcasted_iota(I32, (MOE_EXPERTS, n_tiles), 1).astype(F32)
    te = jnp.sum((last_col <= t_row).astype(F32), axis=0, keepdims=True)
    te_ref[...] = jnp.minimum(te, MOE_EXPERTS - 1.0).astype(I32)
    nused_ref[...] = jnp.sum(nt, axis=-1, keepdims=True).astype(I32)


def _moe_plan(cnt, ids, rank):
    n = ids.shape[0]
    tm = ROUTER_TM if n % ROUTER_TM == 0 else n
    n_tiles = _moe_num_tiles(n)
    return pl.pallas_call(
        functools.partial(_moe_plan_kernel, n_tiles=n_tiles),
        grid=(n // tm,),
        in_specs=[
            pl.BlockSpec((1, MOE_EXPERTS), lambda i: (0, 0)),
            pl.BlockSpec((tm, 2), lambda i: (i, 0)),
            pl.BlockSpec((tm, 2), lambda i: (i, 0)),
        ],
        out_specs=[
            pl.BlockSpec((tm, 2), lambda i: (i, 0)),
            pl.BlockSpec((1, n_tiles), lambda i: (0, 0)),
            pl.BlockSpec((1, 1), lambda i: (0, 0)),
        ],
        out_shape=[
            jax.ShapeDtypeStruct((n, 2), I32),
            jax.ShapeDtypeStruct((1, n_tiles), I32),
            jax.ShapeDtypeStruct((1, 1), I32),
        ],
        compiler_params=_cparams("arbitrary"),
        name="moe_plan",
    )(cnt, ids, rank)


def _row_copy(src_ref, src_row, dst_ref, dst_row, sem):
    return pltpu.make_async_copy(src_ref.at[pl.ds(src_row, 1)], dst_ref.at[pl.ds(dst_row, 1)], sem)


def _moe_dispatch_kernel(pos_ref, x_ref, xs_in_hbm, xs_hbm, sem):
    del xs_in_hbm
    tm = pos_ref.shape[0] // 2

    def start(n, c):
        _row_copy(x_ref, n, xs_hbm, pos_ref[2 * n], sem).start()
        _row_copy(x_ref, n, xs_hbm, pos_ref[2 * n + 1], sem).start()
        return c

    lax.fori_loop(0, tm, start, 0, unroll=8)

    def wait(n, c):
        _row_copy(x_ref, 0, xs_hbm, 0, sem).wait()
        return c

    lax.fori_loop(0, 2 * tm, wait, 0, unroll=8)


def _moe_dispatch(pos_flat, x, n_rows):
    n = x.shape[0]
    tm = ROUTER_TM if n % ROUTER_TM == 0 else n
    xs0 = jnp.zeros((n_rows, D_MODEL), F32)
    return pl.pallas_call(
        _moe_dispatch_kernel,
        grid=(n // tm,),
        in_specs=[
            pl.BlockSpec((2 * tm,), lambda i: (i,), memory_space=pltpu.SMEM),
            pl.BlockSpec((tm, D_MODEL), lambda i: (i, 0)),
            pl.BlockSpec(memory_space=pl.ANY),
        ],
        out_specs=pl.BlockSpec(memory_space=pl.ANY),
        out_shape=jax.ShapeDtypeStruct((n_rows, D_MODEL), F32),
        scratch_shapes=[pltpu.SemaphoreType.DMA(())],
        input_output_aliases={2: 0},
        compiler_params=_cparams("arbitrary"),
        name="moe_dispatch",
    )(pos_flat, x, xs0)


def _moe_gmm_kernel(te_ref, nused_ref, xs_ref, wg_ref, wu_ref, wd_ref, o_ref, wgb_ref, wub_ref, wdb_ref):
    i = pl.program_id(0)
    used = i < nused_ref[0]
    new_expert = jnp.logical_or(i == 0, te_ref[i] != te_ref[jnp.maximum(i - 1, 0)])

    @pl.when(jnp.logical_and(used, new_expert))
    def _():
        wgb_ref[...] = wg_ref[...].astype(BF16)
        wub_ref[...] = wu_ref[...].astype(BF16)
        wdb_ref[...] = wd_ref[...].astype(BF16)

    @pl.when(used)
    def _():
        xb = xs_ref[...].astype(BF16)
        g = jnp.dot(xb, wgb_ref[...], preferred_element_type=F32)
        u = jnp.dot(xb, wub_ref[...], preferred_element_type=F32)
        h = (_silu(g) * u).astype(BF16)
        o_ref[...] = jnp.dot(h, wdb_ref[...], preferred_element_type=F32)

    @pl.when(jnp.logical_not(used))
    def _():
        o_ref[...] = jnp.zeros_like(o_ref)


def _moe_gmm(te, nused, xs, p, layer):
    n_rows = xs.shape[0]
    n_tiles = n_rows // MOE_TM
    wspec = lambda a, b: pl.BlockSpec((None, None, a, b), lambda i, te, nu: (layer, te[i], 0, 0))
    return pl.pallas_call(
        _moe_gmm_kernel,
        grid_spec=pltpu.PrefetchScalarGridSpec(
            num_scalar_prefetch=2,
            grid=(n_tiles,),
            in_specs=[
                pl.BlockSpec((MOE_TM, D_MODEL), lambda i, te, nu: (jnp.minimum(i, nu[0] - 1), 0)),
                wspec(D_MODEL, MOE_D_FF),
                wspec(D_MODEL, MOE_D_FF),
                wspec(MOE_D_FF, D_MODEL),
            ],
            out_specs=pl.BlockSpec((MOE_TM, D_MODEL), lambda i, te, nu: (i, 0)),
            scratch_shapes=[
                pltpu.VMEM((D_MODEL, MOE_D_FF), BF16),
                pltpu.VMEM((D_MODEL, MOE_D_FF), BF16),
                pltpu.VMEM((MOE_D_FF, D_MODEL), BF16),
            ],
        ),
        out_shape=jax.ShapeDtypeStruct((n_rows, D_MODEL), F32),
        compiler_params=_cparams("arbitrary"),
        name="moe_gmm",
    )(te, nused, xs, p["moe_w_gate"], p["moe_w_up"], p["moe_w_down"])


def _moe_combine_ln_kernel(pos_ref, o_hbm, wts_ref, res_ref, g_ref, b_ref, xo_ref, xbo_ref, buf_ref, sem):
    tm = res_ref.shape[0]

    def start(n, c):
        _row_copy(o_hbm, pos_ref[2 * n], buf_ref.at[0], n, sem).start()
        _row_copy(o_hbm, pos_ref[2 * n + 1], buf_ref.at[1], n, sem).start()
        return c

    lax.fori_loop(0, tm, start, 0, unroll=8)

    def wait(n, c):
        _row_copy(o_hbm, 0, buf_ref.at[0], 0, sem).wait()
        return c

    lax.fori_loop(0, 2 * tm, wait, 0, unroll=8)
    w = wts_ref[...]
    y = w[:, 0:1] * buf_ref[0] + w[:, 1:2] * buf_ref[1]
    out = _layer_norm_rows(DN_ALPHA * res_ref[...] + y, g_ref[...], b_ref[...])
    xo_ref[...] = out
    xbo_ref[...] = out.astype(BF16)


def _moe_combine_ln(pos_flat, o, wts, res, g, b, layer):
    n = res.shape[0]
    tm = ROUTER_TM if n % ROUTER_TM == 0 else n
    return pl.pallas_call(
        _moe_combine_ln_kernel,
        grid=(n // tm,),
        in_specs=[
            pl.BlockSpec((2 * tm,), lambda i: (i,), memory_space=pltpu.SMEM),
            pl.BlockSpec(memory_space=pl.ANY),
            pl.BlockSpec((tm, 2), lambda i: (i, 0)),
            pl.BlockSpec((tm, D_MODEL), lambda i: (i, 0)),
            pl.BlockSpec((None, 1, D_MODEL), lambda i: (layer, 0, 0)),
            pl.BlockSpec((None, 1, D_MODEL), lambda i: (layer, 0, 0)),
        ],
        out_specs=[
            pl.BlockSpec((tm, D_MODEL), lambda i: (i, 0)),
            pl.BlockSpec((tm, D_MODEL), lambda i: (i, 0)),
        ],
        out_shape=[
            jax.ShapeDtypeStruct((n, D_MODEL), F32),
            jax.ShapeDtypeStruct((n, D_MODEL), BF16),
        ],
        scratch_shapes=[pltpu.VMEM((2, tm, D_MODEL), F32), pltpu.SemaphoreType.DMA(())],
        compiler_params=_cparams("arbitrary"),
        name="moe_combine_ln",
    )(pos_flat, o, wts, res, _rows3(g), _rows3(b))


def _moe_layer(x, xb, p, layer):
    n = x.shape[0]
    ids, wts, rank, cnt = _moe_route(xb, p, layer)
    pos, te, nused = _moe_plan(cnt, ids, rank)
    pos_flat = pos.reshape(2 * n)
    xs = _moe_dispatch(pos_flat, x, _moe_num_tiles(n) * MOE_TM)
    o = _moe_gmm(te.reshape(-1), nused.reshape(-1), xs, p, layer)
    return _moe_combine_ln(pos_flat, o, wts, x, p["ln_ffn_g"], p["ln_ffn_b"], layer)


GDN_QK_HEADS = 16
GDN_V_HEADS = 32
GDN_HEAD = 128
GDN_QK_DIM = GDN_QK_HEADS * GDN_HEAD
GDN_V_DIM = GDN_V_HEADS * GDN_HEAD
GDN_CONV_DIM = 2 * GDN_QK_DIM + GDN_V_DIM
GDN_CHUNK = 64
GDN_HP_PROMPT = 4
GDN_HP_SAMPLE = 16
CONV_CB = 512


def _conv_act_kernel(u_ref, cw_ref, cb_ref, o_ref, *, period, masked, n_q_blocks, n_k_blocks):
    u = u_ref[...]
    rows, cblk = u.shape
    t_idx = lax.broadcasted_iota(I32, (rows, 1), 0) % period
    cw = cw_ref[...]
    acc = u * cw[CONV_K - 1:CONV_K, :]
    for s in range(1, CONV_K):
        acc = acc + _shift_rows(u, s, t_idx, masked) * cw[CONV_K - 1 - s:CONV_K - s, :]
    y = _silu(acc + cb_ref[...])
    if n_q_blocks or n_k_blocks:
        n = pl.program_id(1)
        is_q = n < n_q_blocks
        is_k = jnp.logical_and(n >= n_q_blocks, n < n_q_blocks + n_k_blocks)
        parts = []
        for hh in range(cblk // GDN_HEAD):
            yh = y[:, hh * GDN_HEAD:(hh + 1) * GDN_HEAD]
            rs = lax.rsqrt(jnp.sum(yh * yh, axis=-1, keepdims=True) + RMS_EPS)
            scale = jnp.where(is_q, rs * GDN_HEAD ** -0.5, jnp.where(is_k, rs, 1.0))
            parts.append(yh * scale)
        y = jnp.concatenate(parts, axis=1)
    o_ref[...] = y


def _conv_act(xw, nseq, period, masked, conv_w, conv_b, j, n_ch, *, n_q_blocks=0, n_k_blocks=0, col0=0):
    rows = period if masked else nseq * period
    grid_b = nseq if masked else 1
    c = CONV_CB
    cb0 = col0 // c
    return pl.pallas_call(
        functools.partial(_conv_act_kernel, period=period, masked=masked, n_q_blocks=n_q_blocks,
                          n_k_blocks=n_k_blocks),
        grid=(grid_b, n_ch // c),
        in_specs=[
            pl.BlockSpec((rows, c), lambda b, n: (b, cb0 + n)),
            pl.BlockSpec((None, CONV_K, c), lambda b, n: (j, 0, n)),
            pl.BlockSpec((None, 1, c), lambda b, n: (j, 0, n)),
        ],
        out_specs=pl.BlockSpec((rows, c), lambda b, n: (b, n)),
        out_shape=jax.ShapeDtypeStruct((nseq * period, n_ch), F32),
        compiler_params=_cparams("arbitrary", "arbitrary"),
        name="conv_act",
    )(xw, conv_w, conv_b)


def _scan_rows_add(g, t_idx, chunk):
    d = 1
    while d < chunk:
        g = g + jnp.where(t_idx >= d, pltpu.roll(g, d, axis=0), 0.0)
        d *= 2
    return g


def _gdn_gates_kernel(xb_ref, w_ref, alog_ref, dtb_ref, bg_ref, *, chunk):
    ba = jnp.dot(xb_ref[...], w_ref[...].astype(BF16), preferred_element_type=F32)
    b = ba[:, :GDN_V_HEADS]
    a = ba[:, GDN_V_HEADS:]
    beta = _sigmoid(b)
    g = -jnp.exp(alog_ref[...]) * _softplus(a + dtb_ref[...])
    t_idx = lax.broadcasted_iota(I32, (g.shape[0], 1), 0) % chunk
    bg_ref[...] = jnp.concatenate([beta, _scan_rows_add(g, t_idx, chunk)], axis=1)


def _gdn_gates(xb, row0, n_rows, w_ba, a_log, dt_bias, j, chunk):
    tm = _row_tile_small(n_rows)
    tm = tm if tm % chunk == 0 and row0 % tm == 0 else n_rows
    r0 = row0 // tm
    return pl.pallas_call(
        functools.partial(_gdn_gates_kernel, chunk=chunk),
        grid=(n_rows // tm,),
        in_specs=[
            pl.BlockSpec((tm, D_MODEL), lambda i: (r0 + i, 0)),
            pl.BlockSpec((D_MODEL, 2 * GDN_V_HEADS), lambda i: (0, 0)),
            pl.BlockSpec((None, 1, GDN_V_HEADS), lambda i: (j, 0, 0)),
            pl.BlockSpec((None, 1, GDN_V_HEADS), lambda i: (j, 0, 0)),
        ],
        out_specs=pl.BlockSpec((tm, 2 * GDN_V_HEADS), lambda i: (i, 0)),
        out_shape=jax.ShapeDtypeStruct((n_rows, 2 * GDN_V_HEADS), F32),
        compiler_params=_cparams("arbitrary"),
        name="gdn_gates",
    )(xb, w_ba, _rows3(a_log), _rows3(dt_bias))


def _split_bf16(a):
    hi = a.astype(BF16)
    return hi, (a - hi.astype(F32)).astype(BF16)


def _dot3s(a_split, b_split, fused):
    a_hi, a_lo = a_split
    b_hi, b_lo = b_split
    d = functools.partial(jnp.dot, preferred_element_type=F32)
    if fused:
        return d(jnp.concatenate([a_hi, a_hi, a_lo], axis=1), jnp.concatenate([b_hi, b_lo, b_hi], axis=0))
    return d(a_hi, b_hi) + (d(a_hi, b_lo) + d(a_lo, b_hi))


def _dot_nt(a, b):
    return lax.dot_general(a, b, (((1,), (1,)), ((), ())), preferred_element_type=F32)


def _dot_tn(a, b):
    return lax.dot_general(a, b, (((0,), (0,)), ((), ())), preferred_element_type=F32)


def _gdn_chunk_kernel(*refs, L, hp, has_state):
    if has_state:
        q_ref, k_ref, v_ref, z_ref, bg_ref, nw_ref, s0_ref, yg_ref, s_ref = refs
    else:
        q_ref, k_ref, v_ref, z_ref, bg_ref, nw_ref, yg_ref, s_ref = refs
        s0_ref = None

    @pl.when(pl.program_id(2) == 0)
    def _():
        if has_state:
            s_ref[...] = s0_ref[...]
        else:
            s_ref[...] = jnp.zeros_like(s_ref)

    hd = GDN_HEAD
    heads = range(2 * hp)
    fused = L % 64 == 0
    bg = bg_ref[...]
    lane = lax.broadcasted_iota(I32, bg.shape, 1)
    r_i = lax.broadcasted_iota(I32, (L, L), 0)
    c_i = lax.broadcasted_iota(I32, (L, L), 1)
    causal = c_i <= r_i
    strict = c_i < r_i
    diag = c_i == r_i
    eye = diag.astype(F32)
    nw = nw_ref[...]
    h0 = pl.program_id(1) * (2 * hp)

    q = [q_ref[:, jp * hd:(jp + 1) * hd] for jp in range(hp)]
    k = [k_ref[:, jp * hd:(jp + 1) * hd] for jp in range(hp)]
    kb = [x.astype(BF16) for x in k]
    kk = [_dot_nt(kb[jp], kb[jp]) for jp in range(hp)]
    qk0 = [_dot_nt(q[jp].astype(BF16), kb[jp]) for jp in range(hp)]
    beta = [jnp.sum(jnp.where(lane == h0 + hv, bg, 0.0), axis=-1, keepdims=True) for hv in heads]
    gc = [jnp.sum(jnp.where(lane == GDN_V_HEADS + h0 + hv, bg, 0.0), axis=-1, keepdims=True) for hv in heads]
    g_row = [jnp.sum(jnp.where(diag, gc[hv], 0.0), axis=0, keepdims=True) for hv in heads]
    decay = [jnp.where(causal, jnp.exp(jnp.where(causal, gc[hv] - g_row[hv], 0.0)), 0.0) for hv in heads]
    a_mat = [jnp.where(strict, beta[hv] * kk[hv // 2] * decay[hv], 0.0) for hv in heads]
    t_inv = [eye - a_mat[hv] for hv in heads]
    sp = [_split_bf16(a_mat[hv]) for hv in heads]
    m = 2
    while m < L:
        pw = [_dot3s(sp[hv], sp[hv], fused) for hv in heads]
        sp = [_split_bf16(pw[hv]) for hv in heads]
        t_inv = [t_inv[hv] + _dot3s(_split_bf16(t_inv[hv]), sp[hv], fused) for hv in heads]
        m *= 2
    e_g = [jnp.exp(gc[hv]) for hv in heads]
    g_last = [gc[hv][L - 1:L, :] for hv in heads]
    v = [v_ref[:, hv * hd:(hv + 1) * hd] for hv in heads]
    rhs = [jnp.concatenate([k[hv // 2] * (beta[hv] * e_g[hv]), v[hv] * beta[hv]], axis=1) for hv in heads]
    wu = [_dot3s(_split_bf16(t_inv[hv]), _split_bf16(rhs[hv]), fused) for hv in heads]
    s = [s_ref[hv] for hv in heads]
    sb = [x.astype(BF16) for x in s]
    wq = [jnp.concatenate([wu[hv][:, :hd].astype(BF16), (q[hv // 2] * e_g[hv]).astype(BF16)], axis=0)
          for hv in heads]
    r = [jnp.dot(wq[hv], sb[hv], preferred_element_type=F32) for hv in heads]
    ub = [(wu[hv][:, hd:] - r[hv][:L]).astype(BF16) for hv in heads]
    o = [r[hv][L:] + jnp.dot((qk0[hv // 2] * decay[hv]).astype(BF16), ub[hv], preferred_element_type=F32)
         for hv in heads]
    k_dec = [(k[hv // 2] * jnp.exp(g_last[hv] - gc[hv])).astype(BF16) for hv in heads]
    for hv in heads:
        s_ref[hv] = jnp.exp(g_last[hv]) * s[hv] + _dot_tn(k_dec[hv], ub[hv])
    for hv in heads:
        on = o[hv] * lax.rsqrt(jnp.mean(o[hv] * o[hv], axis=-1, keepdims=True) + RMS_EPS) * nw
        z = z_ref[:, hv * hd:(hv + 1) * hd]
        yg_ref[:, hv * hd:(hv + 1) * hd] = (on * _silu(z)).astype(BF16)


def _gdn_core(qkv, xw, bg, norm_w, j, s0, nseq, n_chunks, L, hp):
    hd = GDN_HEAD
    row = lambda b, g, n: b * n_chunks + n
    in_specs = [
        pl.BlockSpec((L, hp * hd), lambda b, g, n: (row(b, g, n), g)),
        pl.BlockSpec((L, hp * hd), lambda b, g, n: (row(b, g, n), GDN_QK_DIM // (hp * hd) + g)),
        pl.BlockSpec((L, 2 * hp * hd), lambda b, g, n: (row(b, g, n), 2 * GDN_QK_DIM // (2 * hp * hd) + g)),
        pl.BlockSpec((L, 2 * hp * hd), lambda b, g, n: (row(b, g, n), GDN_CONV_DIM // (2 * hp * hd) + g)),
        pl.BlockSpec((L, 2 * GDN_V_HEADS), lambda b, g, n: (row(b, g, n), 0)),
        pl.BlockSpec((None, 1, hd), lambda b, g, n: (j, 0, 0)),
    ]
    args = [qkv, qkv, qkv, xw, bg, _rows3(norm_w)]
    s_spec = pl.BlockSpec((None, 2 * hp, hd, hd), lambda b, g, n: (b, g, 0, 0))
    if s0 is not None:
        in_specs.append(s_spec)
        args.append(s0)
    return pl.pallas_call(
        functools.partial(_gdn_chunk_kernel, L=L, hp=hp, has_state=s0 is not None),
        grid=(nseq, GDN_QK_HEADS // hp, n_chunks),
        in_specs=in_specs,
        out_specs=[pl.BlockSpec((L, 2 * hp * hd), lambda b, g, n: (row(b, g, n), g)), s_spec],
        out_shape=[
            jax.ShapeDtypeStruct((nseq * n_chunks * L, GDN_V_DIM), BF16),
            jax.ShapeDtypeStruct((nseq, GDN_V_HEADS, hd, hd), F32),
        ],
        compiler_params=_cparams("arbitrary", "arbitrary", "arbitrary"),
        name="gdn_core",
    )(*args)


def _gdn_layer(xb_all, x_all, n_prompt_seq, t_prompt, n_sample_seq, t_sample, s0, buf, p, j, layer):
    n_p = n_prompt_seq * t_prompt
    n_s = n_sample_seq * t_sample
    n_main = GDN_CONV_DIM + GDN_V_DIM
    xw = _proj(xb_all, p["gdn_w_in"], j, n_main, tm=_row_tile(xb_all.shape[0]), tn=1024)
    w_ba = p["gdn_w_in"][j, :, n_main:]
    cw, cb = p["gdn_conv_w"], jnp.zeros((p["gdn_conv_w"].shape[0], 1, GDN_CONV_DIM), F32)
    nqb = GDN_QK_DIM // CONV_CB
    qkv_p = _conv_act(xw, n_prompt_seq, t_prompt, True, cw, cb, j, GDN_CONV_DIM, n_q_blocks=nqb, n_k_blocks=nqb)
    nb_p = _tail_rows(xw, n_prompt_seq, t_prompt, CONV_K - 1, 0, GDN_CONV_DIM)
    chunk = math.gcd(t_prompt, GDN_CHUNK)
    bg_p = _gdn_gates(xb_all, 0, n_p, w_ba, p["gdn_a_log"], p["gdn_dt_bias"], j, chunk)
    yg_p, s_p = _gdn_core(qkv_p, xw, bg_p, p["gdn_norm_w"], j, None, n_prompt_seq, t_prompt // chunk, chunk,
                          GDN_HP_PROMPT)
    hist = CONV_K
    period = hist + t_sample
    xs = xw[n_p:].reshape(n_sample_seq, t_sample, n_main)
    hist_rows = jnp.concatenate([jnp.zeros((n_sample_seq, 1, GDN_CONV_DIM), F32), buf], axis=1)
    hist_rows = jnp.concatenate([hist_rows, jnp.zeros((n_sample_seq, hist, GDN_V_DIM), F32)], axis=-1)
    ext = jnp.concatenate([hist_rows, xs], axis=1)
    nb_s = ext[:, period - (CONV_K - 1):, :GDN_CONV_DIM]
    ext = ext.reshape(n_sample_seq * period, n_main)
    qkv_s = _conv_act(ext, n_sample_seq, period, False, cw, cb, j, GDN_CONV_DIM, n_q_blocks=nqb, n_k_blocks=nqb)
    bg_s = _gdn_gates(xb_all, n_p, n_s, w_ba, p["gdn_a_log"], p["gdn_dt_bias"], j, t_sample)
    bg_s = jnp.concatenate([jnp.zeros((n_sample_seq, hist, 2 * GDN_V_HEADS), F32),
                            bg_s.reshape(n_sample_seq, t_sample, 2 * GDN_V_HEADS)], axis=1)
    yg_s, s_s = _gdn_core(qkv_s, ext, bg_s.reshape(n_sample_seq * period, 2 * GDN_V_HEADS), p["gdn_norm_w"], j,
                          s0, n_sample_seq, 1, period, GDN_HP_SAMPLE)
    yg_s = yg_s.reshape(n_sample_seq, period, GDN_V_DIM)[:, hist:].reshape(n_s, GDN_V_DIM)
    yg = jnp.concatenate([yg_p, yg_s], axis=0)
    x_new, xb_new = _outproj_ln(yg, p["gdn_w_out"], j, x_all, p["ln_mix_g"], p["ln_mix_b"], layer,
                                tm=_row_tile_small(yg.shape[0]), tk=512)
    return x_new, xb_new, (s_p, nb_p), (s_s, nb_s)


SSD_D_INNER = 2 * D_MODEL
SSD_HEAD_DIM = 64
SSD_HEADS = SSD_D_INNER // SSD_HEAD_DIM
SSD_GROUPS = 8
SSD_HPG = SSD_HEADS // SSD_GROUPS
SSD_STATE = 128
SSD_CONV_DIM = SSD_D_INNER + 2 * SSD_GROUPS * SSD_STATE
SSD_CHUNK = 128
SSD_GW = SSD_HPG * SSD_HEAD_DIM


def _ssd_gates_kernel(xb_ref, w_ref, alog_ref, dtb_ref, da_ref, *, chunk):
    raw = jnp.dot(xb_ref[...], w_ref[...].astype(BF16), preferred_element_type=F32)
    dt = _softplus(raw + dtb_ref[...])
    a = dt * (-jnp.exp(alog_ref[...]))
    t_idx = lax.broadcasted_iota(I32, (a.shape[0], 1), 0) % chunk
    da_ref[...] = jnp.concatenate([dt, _scan_rows_add(a, t_idx, chunk)], axis=1)


def _ssd_gates(xb, row0, n_rows, w_dt, a_log, dt_bias, j, chunk):
    tm = _row_tile_small(n_rows)
    tm = tm if tm % chunk == 0 and row0 % tm == 0 else n_rows
    r0 = row0 // tm
    return pl.pallas_call(
        functools.partial(_ssd_gates_kernel, chunk=chunk),
        grid=(n_rows // tm,),
        in_specs=[
            pl.BlockSpec((tm, D_MODEL), lambda i: (r0 + i, 0)),
            pl.BlockSpec((D_MODEL, SSD_HEADS), lambda i: (0, 0)),
            pl.BlockSpec((None, 1, SSD_HEADS), lambda i: (j, 0, 0)),
            pl.BlockSpec((None, 1, SSD_HEADS), lambda i: (j, 0, 0)),
        ],
        out_specs=pl.BlockSpec((tm, 2 * SSD_HEADS), lambda i: (i, 0)),
        out_shape=jax.ShapeDtypeStruct((n_rows, 2 * SSD_HEADS), F32),
        compiler_params=_cparams("arbitrary"),
        name="ssd_gates",
    )(xb, w_dt, _rows3(a_log), _rows3(dt_bias))


def _ssd_chunk_kernel(*refs, L, has_state):
    if has_state:
        x_ref, b_ref, c_ref, z_ref, da_ref, dsk_ref, nw_ref, h0_ref, yg_ref, h_ref = refs
    else:
        x_ref, b_ref, c_ref, z_ref, da_ref, dsk_ref, nw_ref, yg_ref, h_ref = refs
        h0_ref = None

    @pl.when(pl.program_id(2) == 0)
    def _():
        if has_state:
            h_ref[...] = h0_ref[...]
        else:
            h_ref[...] = jnp.zeros_like(h_ref)

    g = pl.program_id(1)
    da = da_ref[...]
    da_lane = lax.broadcasted_iota(I32, da.shape, 1)
    dsk = dsk_ref[...]
    dsk_lane = lax.broadcasted_iota(I32, dsk.shape, 1)
    r_i = lax.broadcasted_iota(I32, (L, L), 0)
    c_i = lax.broadcasted_iota(I32, (L, L), 1)
    causal = c_i <= r_i
    diag = c_i == r_i
    pd = 2 * SSD_HEAD_DIM
    first_half = lax.broadcasted_iota(I32, (1, pd), 1) < SSD_HEAD_DIM
    top_half = lax.broadcasted_iota(I32, (pd, 1), 0) < SSD_HEAD_DIM
    bmb = b_ref[...].astype(BF16)
    cmb = c_ref[...].astype(BF16)
    cb = _dot_nt(cmb, bmb)
    ys = []
    for pp in range(SSD_HPG // 2):
        x = x_ref[:, pp * pd:(pp + 1) * pd]
        per_head = []
        for jv in range(2):
            head = g * SSD_HPG + 2 * pp + jv
            dt = jnp.sum(jnp.where(da_lane == head, da, 0.0), axis=-1, keepdims=True)
            ac = jnp.sum(jnp.where(da_lane == SSD_HEADS + head, da, 0.0), axis=-1, keepdims=True)
            ac_row = jnp.sum(jnp.where(diag, ac, 0.0), axis=0, keepdims=True)
            decay = jnp.where(causal, jnp.exp(jnp.where(causal, ac - ac_row, 0.0)), 0.0)
            ac_last = ac[L - 1:L, :]
            dskip = jnp.sum(jnp.where(dsk_lane == head, dsk, 0.0), axis=-1, keepdims=True)
            per_head.append((dt, ac, (cb * decay).astype(BF16), jnp.exp(ac_last - ac), jnp.exp(ac_last), dskip))
        (dt_e, ac_e, m_e, te_e, cd_e, ds_e), (dt_o, ac_o, m_o, te_o, cd_o, ds_o) = per_head
        xdt = x * jnp.where(first_half, dt_e, dt_o)
        y_diag = jnp.dot(m_e, jnp.where(first_half, xdt, 0.0).astype(BF16), preferred_element_type=F32) + jnp.dot(
            m_o, jnp.where(first_half, 0.0, xdt).astype(BF16), preferred_element_type=F32)
        states = _dot_tn((xdt * jnp.where(first_half, te_e, te_o)).astype(BF16), bmb)
        h_pair = jnp.concatenate([h_ref[2 * pp], h_ref[2 * pp + 1]], axis=0)
        y_off = _dot_nt(cmb, h_pair.astype(BF16)) * jnp.where(first_half, jnp.exp(ac_e), jnp.exp(ac_o))
        h_new = jnp.where(top_half, cd_e, cd_o) * h_pair + states
        h_ref[2 * pp] = h_new[:SSD_HEAD_DIM]
        h_ref[2 * pp + 1] = h_new[SSD_HEAD_DIM:]
        ys.append(y_diag + y_off + jnp.where(first_half, ds_e, ds_o) * x)
    y = jnp.concatenate(ys, axis=1) * _silu(z_ref[...])
    yn = y * lax.rsqrt(jnp.mean(y * y, axis=-1, keepdims=True) + RMS_EPS) * nw_ref[...]
    yg_ref[...] = yn.astype(BF16)


def _ssd_core(xbc, xw, da, d_skip, norm_w, j, h0, nseq, n_chunks, L):
    gw, st = SSD_GW, SSD_STATE
    row = lambda b, g, n: b * n_chunks + n
    in_specs = [
        pl.BlockSpec((L, gw), lambda b, g, n: (row(b, g, n), g)),
        pl.BlockSpec((L, st), lambda b, g, n: (row(b, g, n), SSD_D_INNER // st + g)),
        pl.BlockSpec((L, st), lambda b, g, n: (row(b, g, n), SSD_D_INNER // st + SSD_GROUPS + g)),
        pl.BlockSpec((L, gw), lambda b, g, n: (row(b, g, n), g)),
        pl.BlockSpec((L, 2 * SSD_HEADS), lambda b, g, n: (row(b, g, n), 0)),
        pl.BlockSpec((None, 1, SSD_HEADS), lambda b, g, n: (j, 0, 0)),
        pl.BlockSpec((None, 1, gw), lambda b, g, n: (j, 0, g)),
    ]
    args = [xbc, xbc, xbc, xw, da, _rows3(d_skip), _rows3(norm_w)]
    h_spec = pl.BlockSpec((None, SSD_HPG, SSD_HEAD_DIM, st), lambda b, g, n: (b, g, 0, 0))
    if h0 is not None:
        in_specs.append(h_spec)
        args.append(h0)
    return pl.pallas_call(
        functools.partial(_ssd_chunk_kernel, L=L, has_state=h0 is not None),
        grid=(nseq, SSD_GROUPS, n_chunks),
        in_specs=in_specs,
        out_specs=[pl.BlockSpec((L, gw), lambda b, g, n: (row(b, g, n), g)), h_spec],
        out_shape=[
            jax.ShapeDtypeStruct((nseq * n_chunks * L, SSD_D_INNER), BF16),
            jax.ShapeDtypeStruct((nseq, SSD_HEADS, SSD_HEAD_DIM, st), F32),
        ],
        compiler_params=_cparams("arbitrary", "arbitrary", "arbitrary"),
        name="ssd_core",
    )(*args)


def _ssd_layer(xb_all, x_all, n_prompt_seq, t_prompt, n_sample_seq, t_sample, h0, buf, p, j, layer):
    n_p = n_prompt_seq * t_prompt
    n_s = n_sample_seq * t_sample
    n_main = SSD_D_INNER + SSD_CONV_DIM
    xw = _proj(xb_all, p["ssd_w_in"], j, n_main, tm=_row_tile(xb_all.shape[0]), tn=1024)
    w_dt = p["ssd_w_in"][j, :, n_main:]
    cw, cb = p["ssd_conv_w"], _rows3(p["ssd_conv_b"])
    xbc_p = _conv_act(xw, n_prompt_seq, t_prompt, True, cw, cb, j, SSD_CONV_DIM, col0=SSD_D_INNER)
    nb_p = _tail_rows(xw, n_prompt_seq, t_prompt, CONV_K - 1, SSD_D_INNER, n_main)
    chunk = math.gcd(t_prompt, SSD_CHUNK)
    da_p = _ssd_gates(xb_all, 0, n_p, w_dt, p["ssd_a_log"], p["ssd_dt_bias"], j, chunk)
    yg_p, h_p = _ssd_core(xbc_p, xw, da_p, p["ssd_d"], p["ssd_norm_w"], j, None, n_prompt_seq, t_prompt // chunk,
                          chunk)
    hist = CONV_K
    period = hist + t_sample
    xs = xw[n_p:].reshape(n_sample_seq, t_sample, n_main)
    hist_rows = jnp.concatenate([jnp.zeros((n_sample_seq, 1, SSD_CONV_DIM), F32), buf], axis=1)
    hist_rows = jnp.concatenate([jnp.zeros((n_sample_seq, hist, SSD_D_INNER), F32), hist_rows], axis=-1)
    ext = jnp.concatenate([hist_rows, xs], axis=1)
    nb_s = ext[:, period - (CONV_K - 1):, SSD_D_INNER:]
    ext = ext.reshape(n_sample_seq * period, n_main)
    xbc_s = _conv_act(ext, n_sample_seq, period, False, cw, cb, j, SSD_CONV_DIM, col0=SSD_D_INNER)
    da_s = _ssd_gates(xb_all, n_p, n_s, w_dt, p["ssd_a_log"], p["ssd_dt_bias"], j, t_sample)
    da_s = jnp.concatenate([jnp.zeros((n_sample_seq, hist, 2 * SSD_HEADS), F32),
                            da_s.reshape(n_sample_seq, t_sample, 2 * SSD_HEADS)], axis=1)
    yg_s, h_s = _ssd_core(xbc_s, ext, da_s.reshape(n_sample_seq * period, 2 * SSD_HEADS), p["ssd_d"],
                          p["ssd_norm_w"], j, h0, n_sample_seq, 1, period)
    yg_s = yg_s.reshape(n_sample_seq, period, SSD_D_INNER)[:, hist:].reshape(n_s, SSD_D_INNER)
    yg = jnp.concatenate([yg_p, yg_s], axis=0)
    x_new, xb_new = _outproj_ln(yg, p["ssd_w_out"], j, x_all, p["ln_mix_g"], p["ln_mix_b"], layer,
                                tm=_row_tile_small(yg.shape[0]), tk=512)
    return x_new, xb_new, (h_p, nb_p), (h_s, nb_s)


N_MIXERS = 3


def kernel(x_prompt, x_sample, state_rg_h, state_rg_conv, state_gdn_s, state_gdn_conv, state_ssd_h,
           state_ssd_conv, ln_mix_g, ln_mix_b, ln_ffn_g, ln_ffn_b, moe_router_group_w, moe_router_group_b,
           moe_router_expert_w, moe_router_expert_b, moe_w_gate, moe_w_up, moe_w_down, rg_w_in, rg_conv_w,
           rg_conv_b, rg_w_rgate, rg_b_rgate, rg_w_igate, rg_b_igate, rg_lambda, rg_w_out, gdn_w_in, gdn_conv_w,
           gdn_a_log, gdn_dt_bias, gdn_norm_w, gdn_w_out, ssd_w_in, ssd_conv_w, ssd_conv_b, ssd_a_log,
           ssd_dt_bias, ssd_d, ssd_norm_w, ssd_w_out):
    p = dict(
        ln_mix_g=ln_mix_g, ln_mix_b=ln_mix_b, ln_ffn_g=ln_ffn_g, ln_ffn_b=ln_ffn_b,
        moe_router_group_w=moe_router_group_w, moe_router_group_b=moe_router_group_b,
        moe_router_expert_w=moe_router_expert_w, moe_router_expert_b=moe_router_expert_b,
        moe_w_gate=moe_w_gate, moe_w_up=moe_w_up, moe_w_down=moe_w_down,
        rg_w_in=rg_w_in, rg_conv_w=rg_conv_w, rg_conv_b=rg_conv_b, rg_w_rgate=rg_w_rgate,
        rg_b_rgate=rg_b_rgate, rg_w_igate=rg_w_igate, rg_b_igate=rg_b_igate, rg_lambda=rg_lambda,
        rg_w_out=rg_w_out,
        gdn_w_in=gdn_w_in, gdn_conv_w=gdn_conv_w, gdn_a_log=gdn_a_log, gdn_dt_bias=gdn_dt_bias,
        gdn_norm_w=gdn_norm_w, gdn_w_out=gdn_w_out,
        ssd_w_in=ssd_w_in, ssd_conv_w=ssd_conv_w, ssd_conv_b=ssd_conv_b, ssd_a_log=ssd_a_log,
        ssd_dt_bias=ssd_dt_bias, ssd_d=ssd_d, ssd_norm_w=ssd_norm_w, ssd_w_out=ssd_w_out,
    )
    bp, tp, _ = x_prompt.shape
    bs, ts, _ = x_sample.shape
    n_p = bp * tp
    x = jnp.concatenate([x_prompt.reshape(n_p, D_MODEL), x_sample.reshape(bs * ts, D_MODEL)], axis=0)
    xb = x.astype(BF16)
    layers = {0: (_rg_layer, state_rg_h, state_rg_conv), 1: (_gdn_layer, state_gdn_s, state_gdn_conv),
              2: (_ssd_layer, state_ssd_h, state_ssd_conv)}
    new_p = {0: ([], []), 1: ([], []), 2: ([], [])}
    new_s = {0: ([], []), 1: ([], []), 2: ([], [])}
    for i in range(DEPTH):
        kind, j = i % N_MIXERS, i // N_MIXERS
        fn, st, cv = layers[kind]
        x, xb, (st_p, cv_p), (st_s, cv_s) = fn(xb, x, bp, tp, bs, ts, st[j], cv[j], p, j, i)
        new_p[kind][0].append(st_p)
        new_p[kind][1].append(cv_p)
        new_s[kind][0].append(st_s)
        new_s[kind][1].append(cv_s)
        x, xb = _moe_layer(x, xb, p, i)
    outs = [x[:n_p].reshape(bp, tp, D_MODEL), x[n_p:].reshape(bs, ts, D_MODEL)]
    for kind in range(N_MIXERS):
        for which in range(2):
            outs.append(jnp.stack(new_p[kind][which]))
            outs.append(jnp.stack(new_s[kind][which]))
    return tuple(outs)
```

```python
import functools
import math

import jax
import jax.numpy as jnp
from jax import lax
from jax.experimental import pallas as pl
from jax.experimental.pallas import tpu as pltpu

F32 = jnp.float32
BF16 = jnp.bfloat16
I32 = jnp.int32

D_MODEL = 2048
DEPTH = 4
CONV_K = 4
RG_BLOCK = 256
D_RNN = 2560
RG_BLOCKS = D_RNN // RG_BLOCK
RG_C = 8.0
DN_ALPHA = (2.0 * DEPTH) ** 0.25
LN_EPS = 1e-5
RMS_EPS = 1e-6
LANES = 128

V7X_VMEM_BYTES = 64 * 1024 * 1024
VMEM_LIMIT = 56 * 1024 * 1024


def _cparams(*sem):
    return pltpu.CompilerParams(dimension_semantics=sem, vmem_limit_bytes=VMEM_LIMIT)


def _proj_kernel(x_ref, w_ref, o_ref, wb_ref):
    @pl.when(pl.program_id(1) == 0)
    def _():
        wb_ref[...] = w_ref[...].astype(BF16)

    o_ref[...] = jnp.dot(x_ref[...], wb_ref[...], preferred_element_type=F32)


def _proj(xb, w, layer, n_out, *, tm, tn):
    m, k = xb.shape
    return pl.pallas_call(
        _proj_kernel,
        grid=(n_out // tn, m // tm),
        in_specs=[
            pl.BlockSpec((tm, k), lambda j, i: (i, 0)),
            pl.BlockSpec((None, k, tn), lambda j, i: (layer, 0, j)),
        ],
        out_specs=pl.BlockSpec((tm, tn), lambda j, i: (i, j)),
        out_shape=jax.ShapeDtypeStruct((m, n_out), F32),
        scratch_shapes=[pltpu.VMEM((k, tn), BF16)],
        compiler_params=_cparams("arbitrary", "arbitrary"),
        name="proj",
    )(xb, w)


def _rows3(a):
    return a.reshape(a.shape[0], 1, a.shape[1])


def _tail_rows(xw, nseq, period, n_rows, c0, c1):
    return jnp.stack([xw[(b + 1) * period - n_rows:(b + 1) * period, c0:c1] for b in range(nseq)])


def _layer_norm_rows(z, g, b):
    mu = jnp.mean(z, axis=-1, keepdims=True)
    zc = z - mu
    var = jnp.mean(zc * zc, axis=-1, keepdims=True)
    return zc * lax.rsqrt(var + LN_EPS) * g + b


def _sigmoid(x):
    return 1.0 / (1.0 + jnp.exp(-x))


def _softplus(x):
    return jnp.maximum(x, 0.0) + jnp.log1p(jnp.exp(-jnp.abs(x)))


def _silu(x):
    return x * _sigmoid(x)


def _gelu_tanh(x):
    c = math.sqrt(2.0 / math.pi)
    return 0.5 * x * (1.0 + jnp.tanh(c * (x + 0.044715 * (x * x * x))))


def _neg_expm1(x):
    t = jnp.tanh(0.5 * x)
    return -2.0 * t / (1.0 - t)


def _shift_rows(x, s, t_idx, masked):
    xs = pltpu.roll(x, s, axis=0)
    if masked:
        xs = jnp.where(t_idx >= s, xs, 0.0)
    return xs


def _rg_core_kernel(gate_ref, u_ref, cw_ref, cb_ref, wr_ref, br_ref, wi_ref, bi_ref, lam_ref,
                    yg_ref, hl_ref, *, period, hist):
    u = u_ref[...]
    rows = u.shape[0]
    t_idx = lax.broadcasted_iota(I32, (rows, 1), 0) % period
    cw = cw_ref[...]
    acc = u * cw[CONV_K - 1:CONV_K, :]
    for s in range(1, CONV_K):
        acc = acc + _shift_rows(u, s, t_idx, hist == 0) * cw[CONV_K - 1 - s:CONV_K - s, :]
    uc = acc + cb_ref[...]
    ucb = uc.astype(BF16)
    r = _sigmoid(jnp.dot(ucb, wr_ref[...].astype(BF16), preferred_element_type=F32) + br_ref[...])
    ig = _sigmoid(jnp.dot(ucb, wi_ref[...].astype(BF16), preferred_element_type=F32) + bi_ref[...])
    log_a = (-RG_C) * r * _softplus(-lam_ref[...])
    a = jnp.exp(log_a)
    b = jnp.sqrt(_neg_expm1(2.0 * log_a)) * (ig * uc)
    if hist:
        valid = t_idx >= hist
        a = jnp.where(valid, a, 1.0)
        b = jnp.where(valid, b, 0.0)
        b = jnp.where(t_idx == hist - 1, pltpu.roll(u, hist - 1, axis=0), b)
    d = 1
    while d < period:
        m = t_idx >= d
        b = jnp.where(m, a * pltpu.roll(b, d, axis=0), 0.0) + b
        a = jnp.where(m, a * pltpu.roll(a, d, axis=0), a)
        d *= 2
    h = b
    yg_ref[...] = (_gelu_tanh(gate_ref[...]) * h).astype(BF16)
    if hist:
        hl_ref[...] = h
    else:
        hl_ref[...] = h[rows - 1:rows, :]


def _rg_core(xw, nseq, period, hist, p, j):
    rows = period
    grid_b = nseq
    if hist:
        rows = nseq * period
        grid_b = 1
    c = RG_BLOCK
    vec = lambda name: pl.BlockSpec((None, 1, c), lambda b, n: (j, 0, n))
    in_specs = [
        pl.BlockSpec((rows, c), lambda b, n: (b, n)),
        pl.BlockSpec((rows, c), lambda b, n: (b, RG_BLOCKS + n)),
        pl.BlockSpec((None, CONV_K, c), lambda b, n: (j, 0, n)),
        vec("conv_b"),
        pl.BlockSpec((None, None, c, c), lambda b, n: (j, n, 0, 0)),
        vec("b_r"),
        pl.BlockSpec((None, None, c, c), lambda b, n: (j, n, 0, 0)),
        vec("b_i"),
        vec("lambda"),
    ]
    if hist:
        hl_spec = pl.BlockSpec((rows, c), lambda b, n: (b, n))
        hl_shape = jax.ShapeDtypeStruct((nseq * period, D_RNN), F32)
    else:
        hl_spec = pl.BlockSpec((None, 1, c), lambda b, n: (b, 0, n))
        hl_shape = jax.ShapeDtypeStruct((nseq, 1, D_RNN), F32)
    out_specs = [pl.BlockSpec((rows, c), lambda b, n: (b, n)), hl_spec]
    out_shape = [jax.ShapeDtypeStruct((nseq * period, D_RNN), BF16), hl_shape]
    return pl.pallas_call(
        functools.partial(_rg_core_kernel, period=period, hist=hist),
        grid=(grid_b, RG_BLOCKS),
        in_specs=in_specs,
        out_specs=out_specs,
        out_shape=out_shape,
        compiler_params=_cparams("arbitrary", "arbitrary"),
        name="rg_core",
    )(xw, xw, p["rg_conv_w"], _rows3(p["rg_conv_b"]), p["rg_w_rgate"], _rows3(p["rg_b_rgate"]),
      p["rg_w_igate"], _rows3(p["rg_b_igate"]), _rows3(p["rg_lambda"]))


def _rg_layer(xb_all, x_all, n_prompt_seq, t_prompt, n_sample_seq, t_sample, h0, buf, p, j, layer):
    n_p = n_prompt_seq * t_prompt
    xw = _proj(xb_all, p["rg_w_in"], j, 2 * D_RNN, tm=_row_tile(xb_all.shape[0]), tn=1024)
    yg_p, hl_p = _rg_core(xw, n_prompt_seq, t_prompt, 0, p, j)
    nb_p = _tail_rows(xw, n_prompt_seq, t_prompt, CONV_K - 1, D_RNN, 2 * D_RNN)
    hist = CONV_K
    period = hist + t_sample
    xs = xw[n_p:].reshape(n_sample_seq, t_sample, 2 * D_RNN)
    hist_u = jnp.concatenate([h0[:, None, :], buf], axis=1)
    hist_rows = jnp.concatenate([jnp.zeros_like(hist_u), hist_u], axis=-1)
    ext = jnp.concatenate([hist_rows, xs], axis=1)
    nb_s = ext[:, period - (CONV_K - 1):, D_RNN:]
    yg_s, h_s = _rg_core(ext.reshape(n_sample_seq * period, 2 * D_RNN), n_sample_seq, period, hist, p, j)
    yg_s = yg_s.reshape(n_sample_seq, period, D_RNN)[:, hist:].reshape(n_sample_seq * t_sample, D_RNN)
    hl_s = h_s.reshape(n_sample_seq, period, D_RNN)[:, period - 1]
    yg = jnp.concatenate([yg_p, yg_s], axis=0)
    z = _proj(yg, p["rg_w_out"], j, D_MODEL, tm=_row_tile(yg.shape[0]), tn=512)
    return z, (hl_p[:, 0, :], nb_p), (hl_s, nb_s)


def _row_tile(m):
    for t in (1088, 1024, 512, 256, 128, 64, 32, 16):
        if m % t == 0:
            return t
    raise ValueError(m)


def _row_tile_small(m):
    for t in (544, 512, 256, 128, 64, 32, 16):
        if m % t == 0:
            return t
    raise ValueError(m)


MOE_GROUPS = 8
MOE_PER_GROUP = 8
MOE_EXPERTS = MOE_GROUPS * MOE_PER_GROUP
MOE_D_FF = D_MODEL // 4
MOE_TM = 256
ROUTER_TM = 512


def _moe_num_tiles(n_tok):
    return (2 * n_tok) // MOE_TM + MOE_EXPERTS


def _first_index_of_max(v, lane):
    vmax = jnp.max(v, axis=-1, keepdims=True)
    idx = jnp.min(jnp.where(v == vmax, lane, v.shape[-1]), axis=-1, keepdims=True)
    return vmax, idx


def _router_kernel(z_ref, res_ref, lg_ref, lb_ref, wg_ref, bg_ref, we_ref, be_ref,
                   xo_ref, ids_ref, wts_ref, rank_ref, cnt_ref, carry_ref):
    i = pl.program_id(0)

    @pl.when(i == 0)
    def _():
        carry_ref[...] = jnp.zeros_like(carry_ref)

    xf = _layer_norm_rows(DN_ALPHA * res_ref[...] + z_ref[...], lg_ref[...], lb_ref[...])
    xo_ref[...] = xf
    x = xf.astype(BF16)
    tm = x.shape[0]
    glog = jnp.dot(x, wg_ref[...].astype(BF16), preferred_element_type=F32) + bg_ref[...]
    elog = jnp.dot(x, we_ref[...].astype(BF16), preferred_element_type=F32) + be_ref[...]
    glane = lax.broadcasted_iota(I32, glog.shape, 1)
    gmax, gi = _first_index_of_max(glog, glane)
    gp = 1.0 / jnp.sum(jnp.exp(glog - gmax), axis=-1, keepdims=True)
    lane = lax.broadcasted_iota(I32, elog.shape, 1)
    in_group = (lane // MOE_PER_GROUP) == gi
    neg = jnp.float32(-jnp.inf)
    m1 = jnp.where(in_group, elog, neg)
    e1, i1 = _first_index_of_max(m1, lane)
    denom = jnp.sum(jnp.exp(m1 - e1), axis=-1, keepdims=True)
    m2 = jnp.where(lane == i1, neg, m1)
    e2, i2 = _first_index_of_max(m2, lane)
    p1 = 1.0 / denom
    p2 = jnp.exp(e2 - e1) / denom
    psum = p1 + p2
    ids_ref[...] = jnp.concatenate([i1, i2], axis=1)
    wts_ref[...] = jnp.concatenate([p1 / psum * gp, p2 / psum * gp], axis=1)

    r_i = lax.broadcasted_iota(I32, (tm, tm), 0)
    c_i = lax.broadcasted_iota(I32, (tm, tm), 1)
    before = (c_i < r_i).astype(BF16)
    oh1 = lane == i1
    oh2 = lane == i2
    carry = carry_ref[...]
    tot1 = jnp.sum(oh1.astype(F32), axis=0, keepdims=True)
    tot2 = jnp.sum(oh2.astype(F32), axis=0, keepdims=True)
    c1 = jnp.dot(before, oh1.astype(BF16), preferred_element_type=F32) + carry
    c2 = jnp.dot(before, oh2.astype(BF16), preferred_element_type=F32) + (carry + tot1)
    r1 = jnp.sum(jnp.where(oh1, c1, 0.0), axis=-1, keepdims=True)
    r2 = jnp.sum(jnp.where(oh2, c2, 0.0), axis=-1, keepdims=True)
    rank_ref[...] = jnp.concatenate([r1, r2], axis=1).astype(I32)
    carry = carry + tot1 + tot2
    carry_ref[...] = carry
    cnt_ref[...] = carry


def _moe_route(z, res, p, layer):
    n = z.shape[0]
    tm = ROUTER_TM if n % ROUTER_TM == 0 else n
    pair = lambda dt: jax.ShapeDtypeStruct((n, 2), dt)
    rows = pl.BlockSpec((tm, D_MODEL), lambda i: (i, 0))
    ln_row = pl.BlockSpec((None, 1, D_MODEL), lambda i: (layer, 0, 0))
    return pl.pallas_call(
        _router_kernel,
        grid=(n // tm,),
        in_specs=[
            rows,
            rows,
            ln_row,
            ln_row,
            pl.BlockSpec((None, D_MODEL, MOE_GROUPS), lambda i: (layer, 0, 0)),
            pl.BlockSpec((None, 1, MOE_GROUPS), lambda i: (layer, 0, 0)),
            pl.BlockSpec((None, D_MODEL, MOE_EXPERTS), lambda i: (layer, 0, 0)),
            pl.BlockSpec((None, 1, MOE_EXPERTS), lambda i: (layer, 0, 0)),
        ],
        out_specs=[
            rows,
            pl.BlockSpec((tm, 2), lambda i: (i, 0)),
            pl.BlockSpec((tm, 2), lambda i: (i, 0)),
            pl.BlockSpec((tm, 2), lambda i: (i, 0)),
            pl.BlockSpec((1, MOE_EXPERTS), lambda i: (0, 0)),
        ],
        out_shape=[jax.ShapeDtypeStruct((n, D_MODEL), F32), pair(I32), pair(F32), pair(I32),
                   jax.ShapeDtypeStruct((1, MOE_EXPERTS), F32)],
        scratch_shapes=[pltpu.VMEM((1, MOE_EXPERTS), F32)],
        compiler_params=_cparams("arbitrary"),
        name="moe_router",
    )(z, res, _rows3(p["ln_mix_g"]), _rows3(p["ln_mix_b"]), p["moe_router_group_w"],
      _rows3(p["moe_router_group_b"]), p["moe_router_expert_w"], _rows3(p["moe_router_expert_b"]))


def _moe_plan_kernel(cnt_ref, ids_ref, rank_ref, pos_ref, te_ref, nused_ref, *, n_tiles):
    cnt = cnt_ref[...]
    nt = jnp.floor((cnt + (MOE_TM - 1)) * (1.0 / MOE_TM))
    e_r = lax.broadcasted_iota(I32, (MOE_EXPERTS, MOE_EXPERTS), 0)
    e_c = lax.broadcasted_iota(I32, (MOE_EXPERTS, MOE_EXPERTS), 1)
    nt_b = nt.astype(BF16)
    first = jnp.dot(nt_b, (e_r < e_c).astype(BF16), preferred_element_type=F32)
    last = first + nt
    ids = ids_ref[...]
    lane = lax.broadcasted_iota(I32, (ids.shape[0], MOE_EXPERTS), 1)
    cols = []
    for k in range(2):
        sel = lane == ids[:, k:k + 1]
        cols.append(jnp.sum(jnp.where(sel, first, 0.0), axis=-1, keepdims=True))
    tile0 = jnp.concatenate(cols, axis=1).astype(I32)
    pos_ref[...] = tile0 * MOE_TM + rank_ref[...]
    last_col = jnp.sum(jnp.where(e_r == e_c, last, 0.0), axis=-1, keepdims=True)
    t_row = lax.broadcasted_iota(I32, (MOE_EXPERTS, n_tiles), 1).astype(F32)
    te = jnp.sum((last_col <= t_row).astype(F32), axis=0, keepdims=True)
    te_ref[...] = jnp.minimum(te, MOE_EXPERTS - 1.0).astype(I32)
    nused_ref[...] = jnp.sum(nt, axis=-1, keepdims=True).astype(I32)


def _moe_plan(cnt, ids, rank):
    n = ids.shape[0]
    tm = ROUTER_TM if n % ROUTER_TM == 0 else n
    n_tiles = _moe_num_tiles(n)
    return pl.pallas_call(
        functools.partial(_moe_plan_kernel, n_tiles=n_tiles),
        grid=(n // tm,),
        in_specs=[
            pl.BlockSpec((1, MOE_EXPERTS), lambda i: (0, 0)),
            pl.BlockSpec((tm, 2), lambda i: (i, 0)),
            pl.BlockSpec((tm, 2), lambda i: (i, 0)),
        ],
        out_specs=[
            pl.BlockSpec((tm, 2), lambda i: (i, 0)),
            pl.BlockSpec((1, n_tiles), lambda i: (0, 0)),
            pl.BlockSpec((1, 1), lambda i: (0, 0)),
        ],
        out_shape=[
            jax.ShapeDtypeStruct((n, 2), I32),
            jax.ShapeDtypeStruct((1, n_tiles), I32),
            jax.ShapeDtypeStruct((1, 1), I32),
        ],
        compiler_params=_cparams("arbitrary"),
        name="moe_plan",
    )(cnt, ids, rank)


def _row_copy(src_ref, src_row, dst_ref, dst_row, sem):
    return pltpu.make_async_copy(src_ref.at[pl.ds(src_row, 1)], dst_ref.at[pl.ds(dst_row, 1)], sem)


def _moe_dispatch_kernel(pos_ref, x_ref, xs_in_hbm, xs_hbm, sem):
    del xs_in_hbm
    tm = pos_ref.shape[0] // 2

    def start(n, c):
        _row_copy(x_ref, n, xs_hbm, pos_ref[2 * n], sem).start()
        _row_copy(x_ref, n, xs_hbm, pos_ref[2 * n + 1], sem).start()
        return c

    lax.fori_loop(0, tm, start, 0, unroll=8)

    def wait(n, c):
        _row_copy(x_ref, 0, xs_hbm, 0, sem).wait()
        return c

    lax.fori_loop(0, 2 * tm, wait, 0, unroll=8)


def _moe_dispatch(pos_flat, x, n_rows):
    n = x.shape[0]
    tm = ROUTER_TM if n % ROUTER_TM == 0 else n
    xs0 = jnp.zeros((n_rows, D_MODEL), F32)
    return pl.pallas_call(
        _moe_dispatch_kernel,
        grid=(n // tm,),
        in_specs=[
            pl.BlockSpec((2 * tm,), lambda i: (i,), memory_space=pltpu.SMEM),
            pl.BlockSpec((tm, D_MODEL), lambda i: (i, 0)),
            pl.BlockSpec(memory_space=pl.ANY),
        ],
        out_specs=pl.BlockSpec(memory_space=pl.ANY),
        out_shape=jax.ShapeDtypeStruct((n_rows, D_MODEL), F32),
        scratch_shapes=[pltpu.SemaphoreType.DMA(())],
        input_output_aliases={2: 0},
        compiler_params=_cparams("arbitrary"),
        name="moe_dispatch",
    )(pos_flat, x, xs0)


def _moe_gmm_kernel(te_ref, nused_ref, xs_ref, wg_ref, wu_ref, wd_ref, o_ref, wgb_ref, wub_ref, wdb_ref):
    i = pl.program_id(0)
    used = i < nused_ref[0]
    new_expert = jnp.logical_or(i == 0, te_ref[i] != te_ref[jnp.maximum(i - 1, 0)])

    @pl.when(jnp.logical_and(used, new_expert))
    def _():
        wgb_ref[...] = wg_ref[...].astype(BF16)
        wub_ref[...] = wu_ref[...].astype(BF16)
        wdb_ref[...] = wd_ref[...].astype(BF16)

    @pl.when(used)
    def _():
        xb = xs_ref[...].astype(BF16)
        g = jnp.dot(xb, wgb_ref[...], preferred_element_type=F32)
        u = jnp.dot(xb, wub_ref[...], preferred_element_type=F32)
        h = (_silu(g) * u).astype(BF16)
        o_ref[...] = jnp.dot(h, wdb_ref[...], preferred_element_type=F32)

    @pl.when(jnp.logical_not(used))
    def _():
        o_ref[...] = jnp.zeros_like(o_ref)


def _moe_gmm(te, nused, xs, p, layer):
    n_rows = xs.shape[0]
    n_tiles = n_rows // MOE_TM
    wspec = lambda a, b: pl.BlockSpec((None, None, a, b), lambda i, te, nu: (layer, te[i], 0, 0))
    return pl.pallas_call(
        _moe_gmm_kernel,
        grid_spec=pltpu.PrefetchScalarGridSpec(
            num_scalar_prefetch=2,
            grid=(n_tiles,),
            in_specs=[
                pl.BlockSpec((MOE_TM, D_MODEL), lambda i, te, nu: (jnp.minimum(i, nu[0] - 1), 0)),
                wspec(D_MODEL, MOE_D_FF),
                wspec(D_MODEL, MOE_D_FF),
                wspec(MOE_D_FF, D_MODEL),
            ],
            out_specs=pl.BlockSpec((MOE_TM, D_MODEL), lambda i, te, nu: (i, 0)),
            scratch_shapes=[
                pltpu.VMEM((D_MODEL, MOE_D_FF), BF16),
                pltpu.VMEM((D_MODEL, MOE_D_FF), BF16),
                pltpu.VMEM((MOE_D_FF, D_MODEL), BF16),
            ],
        ),
        out_shape=jax.ShapeDtypeStruct((n_rows, D_MODEL), F32),
        compiler_params=_cparams("arbitrary"),
        name="moe_gmm",
    )(te, nused, xs, p["moe_w_gate"], p["moe_w_up"], p["moe_w_down"])


def _moe_combine_ln_kernel(pos_ref, o_hbm, wts_ref, res_ref, g_ref, b_ref, xo_ref, xbo_ref, buf_ref, sem):
    tm = res_ref.shape[0]

    def start(n, c):
        _row_copy(o_hbm, pos_ref[2 * n], buf_ref.at[0], n, sem).start()
        _row_copy(o_hbm, pos_ref[2 * n + 1], buf_ref.at[1], n, sem).start()
        return c

    lax.fori_loop(0, tm, start, 0, unroll=8)

    def wait(n, c):
        _row_copy(o_hbm, 0, buf_ref.at[0], 0, sem).wait()
        return c

    lax.fori_loop(0, 2 * tm, wait, 0, unroll=8)
    w = wts_ref[...]
    y = w[:, 0:1] * buf_ref[0] + w[:, 1:2] * buf_ref[1]
    out = _layer_norm_rows(DN_ALPHA * res_ref[...] + y, g_ref[...], b_ref[...])
    xo_ref[...] = out
    xbo_ref[...] = out.astype(BF16)


def _moe_combine_ln(pos_flat, o, wts, res, g, b, layer):
    n = res.shape[0]
    tm = ROUTER_TM if n % ROUTER_TM == 0 else n
    return pl.pallas_call(
        _moe_combine_ln_kernel,
        grid=(n // tm,),
        in_specs=[
            pl.BlockSpec((2 * tm,), lambda i: (i,), memory_space=pltpu.SMEM),
            pl.BlockSpec(memory_space=pl.ANY),
            pl.BlockSpec((tm, 2), lambda i: (i, 0)),
            pl.BlockSpec((tm, D_MODEL), lambda i: (i, 0)),
            pl.BlockSpec((None, 1, D_MODEL), lambda i: (layer, 0, 0)),
            pl.BlockSpec((None, 1, D_MODEL), lambda i: (layer, 0, 0)),
        ],
        out_specs=[
            pl.BlockSpec((tm, D_MODEL), lambda i: (i, 0)),
            pl.BlockSpec((tm, D_MODEL), lambda i: (i, 0)),
        ],
        out_shape=[
            jax.ShapeDtypeStruct((n, D_MODEL), F32),
            jax.ShapeDtypeStruct((n, D_MODEL), BF16),
        ],
        scratch_shapes=[pltpu.VMEM((2, tm, D_MODEL), F32), pltpu.SemaphoreType.DMA(())],
        compiler_params=_cparams("arbitrary"),
        name="moe_combine_ln",
    )(pos_flat, o, wts, res, _rows3(g), _rows3(b))


def _moe_layer(z, res, p, layer):
    n = z.shape[0]
    x, ids, wts, rank, cnt = _moe_route(z, res, p, layer)
    pos, te, nused = _moe_plan(cnt, ids, rank)
    pos_flat = pos.reshape(2 * n)
    xs = _moe_dispatch(pos_flat, x, _moe_num_tiles(n) * MOE_TM)
    o = _moe_gmm(te.reshape(-1), nused.reshape(-1), xs, p, layer)
    return _moe_combine_ln(pos_flat, o, wts, x, p["ln_ffn_g"], p["ln_ffn_b"], layer)


GDN_QK_HEADS = 16
GDN_V_HEADS = 32
GDN_HEAD = 128
GDN_QK_DIM = GDN_QK_HEADS * GDN_HEAD
GDN_V_DIM = GDN_V_HEADS * GDN_HEAD
GDN_CONV_DIM = 2 * GDN_QK_DIM + GDN_V_DIM
GDN_CHUNK = 64
GDN_HP_PROMPT = 4
GDN_HP_SAMPLE = 16
CONV_CB = 512


def _conv_act_kernel(u_ref, cw_ref, cb_ref, o_ref, *, period, masked, n_q_blocks, n_k_blocks):
    u = u_ref[...]
    rows, cblk = u.shape
    t_idx = lax.broadcasted_iota(I32, (rows, 1), 0) % period
    cw = cw_ref[...]
    acc = u * cw[CONV_K - 1:CONV_K, :]
    for s in range(1, CONV_K):
        acc = acc + _shift_rows(u, s, t_idx, masked) * cw[CONV_K - 1 - s:CONV_K - s, :]
    y = _silu(acc + cb_ref[...])
    if n_q_blocks or n_k_blocks:
        n = pl.program_id(1)
        is_q = n < n_q_blocks
        is_k = jnp.logical_and(n >= n_q_blocks, n < n_q_blocks + n_k_blocks)
        parts = []
        for hh in range(cblk // GDN_HEAD):
            yh = y[:, hh * GDN_HEAD:(hh + 1) * GDN_HEAD]
            rs = lax.rsqrt(jnp.sum(yh * yh, axis=-1, keepdims=True) + RMS_EPS)
            scale = jnp.where(is_q, rs * GDN_HEAD ** -0.5, jnp.where(is_k, rs, 1.0))
            parts.append(yh * scale)
        y = jnp.concatenate(parts, axis=1)
    o_ref[...] = y


def _conv_act(xw, nseq, period, masked, conv_w, conv_b, j, n_ch, *, n_q_blocks=0, n_k_blocks=0, col0=0):
    rows = period if masked else nseq * period
    grid_b = nseq if masked else 1
    c = CONV_CB
    cb0 = col0 // c
    return pl.pallas_call(
        functools.partial(_conv_act_kernel, period=period, masked=masked, n_q_blocks=n_q_blocks,
                          n_k_blocks=n_k_blocks),
        grid=(grid_b, n_ch // c),
        in_specs=[
            pl.BlockSpec((rows, c), lambda b, n: (b, cb0 + n)),
            pl.BlockSpec((None, CONV_K, c), lambda b, n: (j, 0, n)),
            pl.BlockSpec((None, 1, c), lambda b, n: (j, 0, n)),
        ],
        out_specs=pl.BlockSpec((rows, c), lambda b, n: (b, n)),
        out_shape=jax.ShapeDtypeStruct((nseq * period, n_ch), F32),
        compiler_params=_cparams("arbitrary", "arbitrary"),
        name="conv_act",
    )(xw, conv_w, conv_b)


def _scan_rows_add(g, t_idx, chunk):
    d = 1
    while d < chunk:
        g = g + jnp.where(t_idx >= d, pltpu.roll(g, d, axis=0), 0.0)
        d *= 2
    return g


def _gdn_gates_kernel(xb_ref, w_ref, alog_ref, dtb_ref, bg_ref, *, chunk):
    ba = jnp.dot(xb_ref[...], w_ref[...].astype(BF16), preferred_element_type=F32)
    b = ba[:, :GDN_V_HEADS]
    a = ba[:, GDN_V_HEADS:2 * GDN_V_HEADS]
    beta = _sigmoid(b)
    g = -jnp.exp(alog_ref[...]) * _softplus(a + dtb_ref[...])
    t_idx = lax.broadcasted_iota(I32, (g.shape[0], 1), 0) % chunk
    bg_ref[...] = jnp.concatenate([beta, _scan_rows_add(g, t_idx, chunk)], axis=1)


def _gdn_gates(xb, row0, n_rows, w_in, a_log, dt_bias, j, chunk):
    tm = _row_tile_small(n_rows)
    tm = tm if tm % chunk == 0 and row0 % tm == 0 else n_rows
    r0 = row0 // tm
    col_blk = (GDN_CONV_DIM + GDN_V_DIM) // LANES
    return pl.pallas_call(
        functools.partial(_gdn_gates_kernel, chunk=chunk),
        grid=(n_rows // tm,),
        in_specs=[
            pl.BlockSpec((tm, D_MODEL), lambda i: (r0 + i, 0)),
            pl.BlockSpec((None, D_MODEL, LANES), lambda i: (j, 0, col_blk)),
            pl.BlockSpec((None, 1, GDN_V_HEADS), lambda i: (j, 0, 0)),
            pl.BlockSpec((None, 1, GDN_V_HEADS), lambda i: (j, 0, 0)),
        ],
        out_specs=pl.BlockSpec((tm, 2 * GDN_V_HEADS), lambda i: (i, 0)),
        out_shape=jax.ShapeDtypeStruct((n_rows, 2 * GDN_V_HEADS), F32),
        compiler_params=_cparams("arbitrary"),
        name="gdn_gates",
    )(xb, w_in, _rows3(a_log), _rows3(dt_bias))


def _split_bf16(a):
    hi = a.astype(BF16)
    return hi, (a - hi.astype(F32)).astype(BF16)


def _dot3s(a_split, b_split, fused):
    a_hi, a_lo = a_split
    b_hi, b_lo = b_split
    d = functools.partial(jnp.dot, preferred_element_type=F32)
    if fused:
        return d(jnp.concatenate([a_hi, a_hi, a_lo], axis=1), jnp.concatenate([b_hi, b_lo, b_hi], axis=0))
    return d(a_hi, b_hi) + (d(a_hi, b_lo) + d(a_lo, b_hi))


def _dot_nt(a, b):
    return lax.dot_general(a, b, (((1,), (1,)), ((), ())), preferred_element_type=F32)


def _dot_tn(a, b):
    return lax.dot_general(a, b, (((0,), (0,)), ((), ())), preferred_element_type=F32)


def _gdn_chunk_kernel(*refs, L, hp, has_state):
    if has_state:
        q_ref, k_ref, v_ref, z_ref, bg_ref, nw_ref, s0_ref, yg_ref, s_ref = refs
    else:
        q_ref, k_ref, v_ref, z_ref, bg_ref, nw_ref, yg_ref, s_ref = refs
        s0_ref = None

    @pl.when(pl.program_id(2) == 0)
    def _():
        if has_state:
            s_ref[...] = s0_ref[...]
        else:
            s_ref[...] = jnp.zeros_like(s_ref)

    hd = GDN_HEAD
    heads = range(2 * hp)
    fused = L % 64 == 0
    bg = bg_ref[...]
    lane = lax.broadcasted_iota(I32, bg.shape, 1)
    r_i = lax.broadcasted_iota(I32, (L, L), 0)
    c_i = lax.broadcasted_iota(I32, (L, L), 1)
    causal = c_i <= r_i
    strict = c_i < r_i
    diag = c_i == r_i
    eye = diag.astype(F32)
    nw = nw_ref[...]
    h0 = pl.program_id(1) * (2 * hp)

    q = [q_ref[:, jp * hd:(jp + 1) * hd] for jp in range(hp)]
    k = [k_ref[:, jp * hd:(jp + 1) * hd] for jp in range(hp)]
    kb = [x.astype(BF16) for x in k]
    kk = [_dot_nt(kb[jp], kb[jp]) for jp in range(hp)]
    qk0 = [_dot_nt(q[jp].astype(BF16), kb[jp]) for jp in range(hp)]
    beta = [jnp.sum(jnp.where(lane == h0 + hv, bg, 0.0), axis=-1, keepdims=True) for hv in heads]
    gc = [jnp.sum(jnp.where(lane == GDN_V_HEADS + h0 + hv, bg, 0.0), axis=-1, keepdims=True) for hv in heads]
    g_row = [jnp.sum(jnp.where(diag, gc[hv], 0.0), axis=0, keepdims=True) for hv in heads]
    decay = [jnp.where(causal, jnp.exp(jnp.where(causal, gc[hv] - g_row[hv], 0.0)), 0.0) for hv in heads]
    a_mat = [jnp.where(strict, beta[hv] * kk[hv // 2] * decay[hv], 0.0) for hv in heads]
    t_inv = [eye - a_mat[hv] for hv in heads]
    sp = [_split_bf16(a_mat[hv]) for hv in heads]
    m = 2
    while m < L:
        pw = [_dot3s(sp[hv], sp[hv], fused) for hv in heads]
        sp = [_split_bf16(pw[hv]) for hv in heads]
        t_inv = [t_inv[hv] + _dot3s(_split_bf16(t_inv[hv]), sp[hv], fused) for hv in heads]
        m *= 2
    e_g = [jnp.exp(gc[hv]) for hv in heads]
    g_last = [gc[hv][L - 1:L, :] for hv in heads]
    v = [v_ref[:, hv * hd:(hv + 1) * hd] for hv in heads]
    rhs = [jnp.concatenate([k[hv // 2] * (beta[hv] * e_g[hv]), v[hv] * beta[hv]], axis=1) for hv in heads]
    wu = [_dot3s(_split_bf16(t_inv[hv]), _split_bf16(rhs[hv]), fused) for hv in heads]
    s = [s_ref[hv] for hv in heads]
    sb = [x.astype(BF16) for x in s]
    wq = [jnp.concatenate([wu[hv][:, :hd].astype(BF16), (q[hv // 2] * e_g[hv]).astype(BF16)], axis=0)
          for hv in heads]
    r = [jnp.dot(wq[hv], sb[hv], preferred_element_type=F32) for hv in heads]
    ub = [(wu[hv][:, hd:] - r[hv][:L]).astype(BF16) for hv in heads]
    o = [r[hv][L:] + jnp.dot((qk0[hv // 2] * decay[hv]).astype(BF16), ub[hv], preferred_element_type=F32)
         for hv in heads]
    k_dec = [(k[hv // 2] * jnp.exp(g_last[hv] - gc[hv])).astype(BF16) for hv in heads]
    for hv in heads:
        s_ref[hv] = jnp.exp(g_last[hv]) * s[hv] + _dot_tn(k_dec[hv], ub[hv])
    for hv in heads:
        on = o[hv] * lax.rsqrt(jnp.mean(o[hv] * o[hv], axis=-1, keepdims=True) + RMS_EPS) * nw
        z = z_ref[:, hv * hd:(hv + 1) * hd]
        yg_ref[:, hv * hd:(hv + 1) * hd] = (on * _silu(z)).astype(BF16)


def _gdn_core(qkv, xw, bg, norm_w, j, s0, nseq, n_chunks, L, hp):
    hd = GDN_HEAD
    row = lambda b, g, n: b * n_chunks + n
    in_specs = [
        pl.BlockSpec((L, hp * hd), lambda b, g, n: (row(b, g, n), g)),
        pl.BlockSpec((L, hp * hd), lambda b, g, n: (row(b, g, n), GDN_QK_DIM // (hp * hd) + g)),
        pl.BlockSpec((L, 2 * hp * hd), lambda b, g, n: (row(b, g, n), 2 * GDN_QK_DIM // (2 * hp * hd) + g)),
        pl.BlockSpec((L, 2 * hp * hd), lambda b, g, n: (row(b, g, n), GDN_CONV_DIM // (2 * hp * hd) + g)),
        pl.BlockSpec((L, 2 * GDN_V_HEADS), lambda b, g, n: (row(b, g, n), 0)),
        pl.BlockSpec((None, 1, hd), lambda b, g, n: (j, 0, 0)),
    ]
    args = [qkv, qkv, qkv, xw, bg, _rows3(norm_w)]
    s_spec = pl.BlockSpec((None, 2 * hp, hd, hd), lambda b, g, n: (b, g, 0, 0))
    if s0 is not None:
        in_specs.append(s_spec)
        args.append(s0)
    return pl.pallas_call(
        functools.partial(_gdn_chunk_kernel, L=L, hp=hp, has_state=s0 is not None),
        grid=(nseq, GDN_QK_HEADS // hp, n_chunks),
        in_specs=in_specs,
        out_specs=[pl.BlockSpec((L, 2 * hp * hd), lambda b, g, n: (row(b, g, n), g)), s_spec],
        out_shape=[
            jax.ShapeDtypeStruct((nseq * n_chunks * L, GDN_V_DIM), BF16),
            jax.ShapeDtypeStruct((nseq, GDN_V_HEADS, hd, hd), F32),
        ],
        compiler_params=_cparams("arbitrary", "arbitrary", "arbitrary"),
        name="gdn_core",
    )(*args)


def _gdn_layer(xb_all, x_all, n_prompt_seq, t_prompt, n_sample_seq, t_sample, s0, buf, p, j, layer):
    n_p = n_prompt_seq * t_prompt
    n_s = n_sample_seq * t_sample
    n_main = GDN_CONV_DIM + GDN_V_DIM
    xw = _proj(xb_all, p["gdn_w_in"], j, n_main, tm=_row_tile(xb_all.shape[0]), tn=1024)
    w_ba = p["gdn_w_in"]
    cw, cb = p["gdn_conv_w"], jnp.zeros((p["gdn_conv_w"].shape[0], 1, GDN_CONV_DIM), F32)
    nqb = GDN_QK_DIM // CONV_CB
    qkv_p = _conv_act(xw, n_prompt_seq, t_prompt, True, cw, cb, j, GDN_CONV_DIM, n_q_blocks=nqb, n_k_blocks=nqb)
    nb_p = _tail_rows(xw, n_prompt_seq, t_prompt, CONV_K - 1, 0, GDN_CONV_DIM)
    chunk = math.gcd(t_prompt, GDN_CHUNK)
    bg_p = _gdn_gates(xb_all, 0, n_p, w_ba, p["gdn_a_log"], p["gdn_dt_bias"], j, chunk)
    yg_p, s_p = _gdn_core(qkv_p, xw, bg_p, p["gdn_norm_w"], j, None, n_prompt_seq, t_prompt // chunk, chunk,
                          GDN_HP_PROMPT)
    hist = CONV_K
    period = hist + t_sample
    xs = xw[n_p:].reshape(n_sample_seq, t_sample, n_main)
    hist_rows = jnp.concatenate([jnp.zeros((n_sample_seq, 1, GDN_CONV_DIM), F32), buf], axis=1)
    hist_rows = jnp.concatenate([hist_rows, jnp.zeros((n_sample_seq, hist, GDN_V_DIM), F32)], axis=-1)
    ext = jnp.concatenate([hist_rows, xs], axis=1)
    nb_s = ext[:, period - (CONV_K - 1):, :GDN_CONV_DIM]
    ext = ext.reshape(n_sample_seq * period, n_main)
    qkv_s = _conv_act(ext, n_sample_seq, period, False, cw, cb, j, GDN_CONV_DIM, n_q_blocks=nqb, n_k_blocks=nqb)
    bg_s = _gdn_gates(xb_all, n_p, n_s, w_ba, p["gdn_a_log"], p["gdn_dt_bias"], j, t_sample)
    bg_s = jnp.concatenate([jnp.zeros((n_sample_seq, hist, 2 * GDN_V_HEADS), F32),
                            bg_s.reshape(n_sample_seq, t_sample, 2 * GDN_V_HEADS)], axis=1)
    yg_s, s_s = _gdn_core(qkv_s, ext, bg_s.reshape(n_sample_seq * period, 2 * GDN_V_HEADS), p["gdn_norm_w"], j,
                          s0, n_sample_seq, 1, period, GDN_HP_SAMPLE)
    yg_s = yg_s.reshape(n_sample_seq, period, GDN_V_DIM)[:, hist:].reshape(n_s, GDN_V_DIM)
    yg = jnp.concatenate([yg_p, yg_s], axis=0)
    z = _proj(yg, p["gdn_w_out"], j, D_MODEL, tm=_row_tile(yg.shape[0]), tn=512)
    return z, (s_p, nb_p), (s_s, nb_s)


SSD_D_INNER = 2 * D_MODEL
SSD_HEAD_DIM = 64
SSD_HEADS = SSD_D_INNER // SSD_HEAD_DIM
SSD_GROUPS = 8
SSD_HPG = SSD_HEADS // SSD_GROUPS
SSD_STATE = 128
SSD_CONV_DIM = SSD_D_INNER + 2 * SSD_GROUPS * SSD_STATE
SSD_CHUNK = 128
SSD_GW = SSD_HPG * SSD_HEAD_DIM
SSD_NG_PROMPT = 2
SSD_NG_SAMPLE = 8


def _ssd_gates_kernel(xb_ref, w_ref, alog_ref, dtb_ref, da_ref, *, chunk):
    raw = jnp.dot(xb_ref[...], w_ref[...].astype(BF16), preferred_element_type=F32)[:, :SSD_HEADS]
    dt = _softplus(raw + dtb_ref[...])
    a = dt * (-jnp.exp(alog_ref[...]))
    t_idx = lax.broadcasted_iota(I32, (a.shape[0], 1), 0) % chunk
    da_ref[...] = jnp.concatenate([dt, _scan_rows_add(a, t_idx, chunk)], axis=1)


def _ssd_gates(xb, row0, n_rows, w_dt, a_log, dt_bias, j, chunk):
    tm = _row_tile_small(n_rows)
    tm = tm if tm % chunk == 0 and row0 % tm == 0 else n_rows
    r0 = row0 // tm
    col_blk = (SSD_D_INNER + SSD_CONV_DIM) // LANES
    return pl.pallas_call(
        functools.partial(_ssd_gates_kernel, chunk=chunk),
        grid=(n_rows // tm,),
        in_specs=[
            pl.BlockSpec((tm, D_MODEL), lambda i: (r0 + i, 0)),
            pl.BlockSpec((None, D_MODEL, LANES), lambda i: (j, 0, col_blk)),
            pl.BlockSpec((None, 1, SSD_HEADS), lambda i: (j, 0, 0)),
            pl.BlockSpec((None, 1, SSD_HEADS), lambda i: (j, 0, 0)),
        ],
        out_specs=pl.BlockSpec((tm, 2 * SSD_HEADS), lambda i: (i, 0)),
        out_shape=jax.ShapeDtypeStruct((n_rows, 2 * SSD_HEADS), F32),
        compiler_params=_cparams("arbitrary"),
        name="ssd_gates",
    )(xb, w_dt, _rows3(a_log), _rows3(dt_bias))


def _ssd_chunk_kernel(*refs, L, ng, has_state):
    if has_state:
        x_ref, b_ref, c_ref, z_ref, da_ref, dsk_ref, nw_ref, h0_ref, yg_ref, h_ref = refs
    else:
        x_ref, b_ref, c_ref, z_ref, da_ref, dsk_ref, nw_ref, yg_ref, h_ref = refs
        h0_ref = None

    @pl.when(pl.program_id(2) == 0)
    def _():
        if has_state:
            h_ref[...] = h0_ref[...]
        else:
            h_ref[...] = jnp.zeros_like(h_ref)

    g0 = pl.program_id(1) * ng
    da = da_ref[...]
    da_lane = lax.broadcasted_iota(I32, da.shape, 1)
    dsk = dsk_ref[...]
    dsk_lane = lax.broadcasted_iota(I32, dsk.shape, 1)
    r_i = lax.broadcasted_iota(I32, (L, L), 0)
    c_i = lax.broadcasted_iota(I32, (L, L), 1)
    causal = c_i <= r_i
    diag = c_i == r_i
    pd = 2 * SSD_HEAD_DIM
    st = SSD_STATE
    ppg = SSD_HPG // 2
    first_half = lax.broadcasted_iota(I32, (1, pd), 1) < SSD_HEAD_DIM
    top_half = lax.broadcasted_iota(I32, (pd, 1), 0) < SSD_HEAD_DIM
    groups = range(ng)
    pairs = range(ng * ppg)
    heads = range(ng * SSD_HPG)
    bmb = [b_ref[:, gi * st:(gi + 1) * st].astype(BF16) for gi in groups]
    cmb = [c_ref[:, gi * st:(gi + 1) * st].astype(BF16) for gi in groups]
    cb = [_dot_nt(cmb[gi], bmb[gi]) for gi in groups]
    head0 = g0 * SSD_HPG
    dt = [jnp.sum(jnp.where(da_lane == head0 + hh, da, 0.0), axis=-1, keepdims=True) for hh in heads]
    ac = [jnp.sum(jnp.where(da_lane == SSD_HEADS + head0 + hh, da, 0.0), axis=-1, keepdims=True) for hh in heads]
    ac_row = [jnp.sum(jnp.where(diag, ac[hh], 0.0), axis=0, keepdims=True) for hh in heads]
    decay = [jnp.where(causal, jnp.exp(jnp.where(causal, ac[hh] - ac_row[hh], 0.0)), 0.0) for hh in heads]
    m_h = [(cb[hh // SSD_HPG] * decay[hh]).astype(BF16) for hh in heads]
    ac_last = [ac[hh][L - 1:L, :] for hh in heads]
    dskip = [jnp.sum(jnp.where(dsk_lane == head0 + hh, dsk, 0.0), axis=-1, keepdims=True) for hh in heads]
    both = lambda vals, pr: jnp.where(first_half, vals[2 * pr], vals[2 * pr + 1])
    x = [x_ref[:, pr * pd:(pr + 1) * pd] for pr in pairs]
    xdt = [x[pr] * both(dt, pr) for pr in pairs]
    y_diag = [jnp.dot(m_h[2 * pr], jnp.where(first_half, xdt[pr], 0.0).astype(BF16), preferred_element_type=F32)
              + jnp.dot(m_h[2 * pr + 1], jnp.where(first_half, 0.0, xdt[pr]).astype(BF16),
                        preferred_element_type=F32) for pr in pairs]
    to_end = [jnp.exp(ac_last[hh] - ac[hh]) for hh in heads]
    states = [_dot_tn((xdt[pr] * both(to_end, pr)).astype(BF16), bmb[pr // ppg]) for pr in pairs]
    h_pair = [jnp.concatenate([h_ref[2 * pr], h_ref[2 * pr + 1]], axis=0) for pr in pairs]
    e_ac = [jnp.exp(ac[hh]) for hh in heads]
    y_off = [_dot_nt(cmb[pr // ppg], h_pair[pr].astype(BF16)) * both(e_ac, pr) for pr in pairs]
    cd = [jnp.exp(ac_last[hh]) for hh in heads]
    for pr in pairs:
        h_new = jnp.where(top_half, cd[2 * pr], cd[2 * pr + 1]) * h_pair[pr] + states[pr]
        h_ref[2 * pr] = h_new[:SSD_HEAD_DIM]
        h_ref[2 * pr + 1] = h_new[SSD_HEAD_DIM:]
    ys = [y_diag[pr] + y_off[pr] + both(dskip, pr) * x[pr] for pr in pairs]
    gw = SSD_GW
    for gi in groups:
        y = jnp.concatenate(ys[gi * ppg:(gi + 1) * ppg], axis=1) * _silu(z_ref[:, gi * gw:(gi + 1) * gw])
        yn = y * lax.rsqrt(jnp.mean(y * y, axis=-1, keepdims=True) + RMS_EPS) * nw_ref[:, gi * gw:(gi + 1) * gw]
        yg_ref[:, gi * gw:(gi + 1) * gw] = yn.astype(BF16)


def _ssd_core(xbc, xw, da, d_skip, norm_w, j, h0, nseq, n_chunks, L, ng):
    gw, st = ng * SSD_GW, ng * SSD_STATE
    row = lambda b, g, n: b * n_chunks + n
    in_specs = [
        pl.BlockSpec((L, gw), lambda b, g, n: (row(b, g, n), g)),
        pl.BlockSpec((L, st), lambda b, g, n: (row(b, g, n), SSD_D_INNER // st + g)),
        pl.BlockSpec((L, st), lambda b, g, n: (row(b, g, n), SSD_D_INNER // st + SSD_GROUPS // ng + g)),
        pl.BlockSpec((L, gw), lambda b, g, n: (row(b, g, n), g)),
        pl.BlockSpec((L, 2 * SSD_HEADS), lambda b, g, n: (row(b, g, n), 0)),
        pl.BlockSpec((None, 1, SSD_HEADS), lambda b, g, n: (j, 0, 0)),
        pl.BlockSpec((None, 1, gw), lambda b, g, n: (j, 0, g)),
    ]
    args = [xbc, xbc, xbc, xw, da, _rows3(d_skip), _rows3(norm_w)]
    h_spec = pl.BlockSpec((None, ng * SSD_HPG, SSD_HEAD_DIM, SSD_STATE), lambda b, g, n: (b, g, 0, 0))
    if h0 is not None:
        in_specs.append(h_spec)
        args.append(h0)
    return pl.pallas_call(
        functools.partial(_ssd_chunk_kernel, L=L, ng=ng, has_state=h0 is not None),
        grid=(nseq, SSD_GROUPS // ng, n_chunks),
        in_specs=in_specs,
        out_specs=[pl.BlockSpec((L, gw), lambda b, g, n: (row(b, g, n), g)), h_spec],
        out_shape=[
            jax.ShapeDtypeStruct((nseq * n_chunks * L, SSD_D_INNER), BF16),
            jax.ShapeDtypeStruct((nseq, SSD_HEADS, SSD_HEAD_DIM, SSD_STATE), F32),
        ],
        compiler_params=_cparams("arbitrary", "arbitrary", "arbitrary"),
        name="ssd_core",
    )(*args)


def _ssd_layer(xb_all, x_all, n_prompt_seq, t_prompt, n_sample_seq, t_sample, h0, buf, p, j, layer):
    n_p = n_prompt_seq * t_prompt
    n_s = n_sample_seq * t_sample
    n_main = SSD_D_INNER + SSD_CONV_DIM
    xw = _proj(xb_all, p["ssd_w_in"], j, n_main, tm=_row_tile(xb_all.shape[0]), tn=1024)
    w_dt = p["ssd_w_in"]
    cw, cb = p["ssd_conv_w"], _rows3(p["ssd_conv_b"])
    xbc_p = _conv_act(xw, n_prompt_seq, t_prompt, True, cw, cb, j, SSD_CONV_DIM, col0=SSD_D_INNER)
    nb_p = _tail_rows(xw, n_prompt_seq, t_prompt, CONV_K - 1, SSD_D_INNER, n_main)
    chunk = math.gcd(t_prompt, SSD_CHUNK)
    da_p = _ssd_gates(xb_all, 0, n_p, w_dt, p["ssd_a_log"], p["ssd_dt_bias"], j, chunk)
    yg_p, h_p = _ssd_core(xbc_p, xw, da_p, p["ssd_d"], p["ssd_norm_w"], j, None, n_prompt_seq, t_prompt // chunk,
                          chunk, SSD_NG_PROMPT)
    hist = CONV_K
    period = hist + t_sample
    xs = xw[n_p:].reshape(n_sample_seq, t_sample, n_main)
    hist_rows = jnp.concatenate([jnp.zeros((n_sample_seq, 1, SSD_CONV_DIM), F32), buf], axis=1)
    hist_rows = jnp.concatenate([jnp.zeros((n_sample_seq, hist, SSD_D_INNER), F32), hist_rows], axis=-1)
    ext = jnp.concatenate([hist_rows, xs], axis=1)
    nb_s = ext[:, period - (CONV_K - 1):, SSD_D_INNER:]
    ext = ext.reshape(n_sample_seq * period, n_main)
    xbc_s = _conv_act(ext, n_sample_seq, period, False, cw, cb, j, SSD_CONV_DIM, col0=SSD_D_INNER)
    da_s = _ssd_gates(xb_all, n_p, n_s, w_dt, p["ssd_a_log"], p["ssd_dt_bias"], j, t_sample)
    da_s = jnp.concatenate([jnp.zeros((n_sample_seq, hist, 2 * SSD_HEADS), F32),
                            da_s.reshape(n_sample_seq, t_sample, 2 * SSD_HEADS)], axis=1)
    yg_s, h_s = _ssd_core(xbc_s, ext, da_s.reshape(n_sample_seq * period, 2 * SSD_HEADS), p["ssd_d"],
                          p["ssd_norm_w"], j, h0, n_sample_seq, 1, period, SSD_NG_SAMPLE)
    yg_s = yg_s.reshape(n_sample_seq, period, SSD_D_INNER)[:, hist:].reshape(n_s, SSD_D_INNER)
    yg = jnp.concatenate([yg_p, yg_s], axis=0)
    z = _proj(yg, p["ssd_w_out"], j, D_MODEL, tm=_row_tile(yg.shape[0]), tn=512)
    return z, (h_p, nb_p), (h_s, nb_s)


N_MIXERS = 3


def kernel(x_prompt, x_sample, state_rg_h, state_rg_conv, state_gdn_s, state_gdn_conv, state_ssd_h,
           state_ssd_conv, ln_mix_g, ln_mix_b, ln_ffn_g, ln_ffn_b, moe_router_group_w, moe_router_group_b,
           moe_router_expert_w, moe_router_expert_b, moe_w_gate, moe_w_up, moe_w_down, rg_w_in, rg_conv_w,
           rg_conv_b, rg_w_rgate, rg_b_rgate, rg_w_igate, rg_b_igate, rg_lambda, rg_w_out, gdn_w_in, gdn_conv_w,
           gdn_a_log, gdn_dt_bias, gdn_norm_w, gdn_w_out, ssd_w_in, ssd_conv_w, ssd_conv_b, ssd_a_log,
           ssd_dt_bias, ssd_d, ssd_norm_w, ssd_w_out):
    p = dict(
        ln_mix_g=ln_mix_g, ln_mix_b=ln_mix_b, ln_ffn_g=ln_ffn_g, ln_ffn_b=ln_ffn_b,
        moe_router_group_w=moe_router_group_w, moe_router_group_b=moe_router_group_b,
        moe_router_expert_w=moe_router_expert_w, moe_router_expert_b=moe_router_expert_b,
        moe_w_gate=moe_w_gate, moe_w_up=moe_w_up, moe_w_down=moe_w_down,
        rg_w_in=rg_w_in, rg_conv_w=rg_conv_w, rg_conv_b=rg_conv_b, rg_w_rgate=rg_w_rgate,
        rg_b_rgate=rg_b_rgate, rg_w_igate=rg_w_igate, rg_b_igate=rg_b_igate, rg_lambda=rg_lambda,
        rg_w_out=rg_w_out,
        gdn_w_in=gdn_w_in, gdn_conv_w=gdn_conv_w, gdn_a_log=gdn_a_log, gdn_dt_bias=gdn_dt_bias,
        gdn_norm_w=gdn_norm_w, gdn_w_out=gdn_w_out,
        ssd_w_in=ssd_w_in, ssd_conv_w=ssd_conv_w, ssd_conv_b=ssd_conv_b, ssd_a_log=ssd_a_log,
        ssd_dt_bias=ssd_dt_bias, ssd_d=ssd_d, ssd_norm_w=ssd_norm_w, ssd_w_out=ssd_w_out,
    )
    bp, tp, _ = x_prompt.shape
    bs, ts, _ = x_sample.shape
    n_p = bp * tp
    x = jnp.concatenate([x_prompt.reshape(n_p, D_MODEL), x_sample.reshape(bs * ts, D_MODEL)], axis=0)
    xb = x.astype(BF16)
    layers = {0: (_rg_layer, state_rg_h, state_rg_conv), 1: (_gdn_layer, state_gdn_s, state_gdn_conv),
              2: (_ssd_layer, state_ssd_h, state_ssd_conv)}
    new_p = {0: ([], []), 1: ([], []), 2: ([], [])}
    new_s = {0: ([], []), 1: ([], []), 2: ([], [])}
    for i in range(DEPTH):
        kind, j = i % N_MIXERS, i // N_MIXERS
        fn, st, cv = layers[kind]
        z, (st_p, cv_p), (st_s, cv_s) = fn(xb, x, bp, tp, bs, ts, st[j], cv[j], p, j, i)
        new_p[kind][0].append(st_p)
        new_p[kind][1].append(cv_p)
        new_s[kind][0].append(st_s)
        new_s[kind][1].append(cv_s)
        x, xb = _moe_layer(z, x, p, i)
    outs = [x[:n_p].reshape(bp, tp, D_MODEL), x[n_p:].reshape(bs, ts, D_MODEL)]
    for kind in range(N_MIXERS):
        for which in range(2):
            outs.append(jnp.stack(new_p[kind][which]))
            outs.append(jnp.stack(new_s[kind][which]))
    return tuple(outs)
```

```python
import functools
import math

import jax
import jax.numpy as jnp
from jax import lax
from jax.experimental import pallas as pl
from jax.experimental.pallas import tpu as pltpu

F32 = jnp.float32
BF16 = jnp.bfloat16
I32 = jnp.int32

D_MODEL = 2048
DEPTH = 4
CONV_K = 4
RG_BLOCK = 256
D_RNN = 2560
RG_BLOCKS = D_RNN // RG_BLOCK
RG_C = 8.0
DN_ALPHA = (2.0 * DEPTH) ** 0.25
LN_EPS = 1e-5
RMS_EPS = 1e-6
LANES = 128

V7X_VMEM_BYTES = 64 * 1024 * 1024
VMEM_LIMIT = 56 * 1024 * 1024


def _cparams(*sem):
    return pltpu.CompilerParams(dimension_semantics=sem, vmem_limit_bytes=VMEM_LIMIT)


def _proj_kernel(x_ref, w_ref, o_ref, wb_ref):
    @pl.when(pl.program_id(1) == 0)
    def _():
        wb_ref[...] = w_ref[...].astype(BF16)

    o_ref[...] = jnp.dot(x_ref[...], wb_ref[...], preferred_element_type=F32)


def _proj(xb, w, layer, n_out, *, tm, tn):
    m, k = xb.shape
    return pl.pallas_call(
        _proj_kernel,
        grid=(n_out // tn, m // tm),
        in_specs=[
            pl.BlockSpec((tm, k), lambda j, i: (i, 0)),
            pl.BlockSpec((None, k, tn), lambda j, i: (layer, 0, j)),
        ],
        out_specs=pl.BlockSpec((tm, tn), lambda j, i: (i, j)),
        out_shape=jax.ShapeDtypeStruct((m, n_out), F32),
        scratch_shapes=[pltpu.VMEM((k, tn), BF16)],
        compiler_params=_cparams("arbitrary", "arbitrary"),
        name="proj",
    )(xb, w)


def _rows3(a):
    return a.reshape(a.shape[0], 1, a.shape[1])


def _tail_rows(xw, nseq, period, n_rows, c0, c1):
    return jnp.stack([xw[(b + 1) * period - n_rows:(b + 1) * period, c0:c1] for b in range(nseq)])


def _layer_norm_rows(z, g, b):
    mu = jnp.mean(z, axis=-1, keepdims=True)
    zc = z - mu
    var = jnp.mean(zc * zc, axis=-1, keepdims=True)
    return zc * lax.rsqrt(var + LN_EPS) * g + b


def _sigmoid(x):
    return 1.0 / (1.0 + jnp.exp(-x))


def _softplus(x):
    return jnp.maximum(x, 0.0) + jnp.log1p(jnp.exp(-jnp.abs(x)))


def _silu(x):
    return x * _sigmoid(x)


def _gelu_tanh(x):
    c = math.sqrt(2.0 / math.pi)
    return 0.5 * x * (1.0 + jnp.tanh(c * (x + 0.044715 * (x * x * x))))


def _neg_expm1(x):
    t = jnp.tanh(0.5 * x)
    return -2.0 * t / (1.0 - t)


def _shift_rows(x, s, t_idx, masked):
    xs = pltpu.roll(x, s, axis=0)
    if masked:
        xs = jnp.where(t_idx >= s, xs, 0.0)
    return xs


def _rg_core_kernel(gate_ref, u_ref, cw_ref, cb_ref, wr_ref, br_ref, wi_ref, bi_ref, lam_ref,
                    yg_ref, hl_ref, *, period, hist):
    u = u_ref[...]
    rows = u.shape[0]
    t_idx = lax.broadcasted_iota(I32, (rows, 1), 0) % period
    cw = cw_ref[...]
    acc = u * cw[CONV_K - 1:CONV_K, :]
    for s in range(1, CONV_K):
        acc = acc + _shift_rows(u, s, t_idx, hist == 0) * cw[CONV_K - 1 - s:CONV_K - s, :]
    uc = acc + cb_ref[...]
    ucb = uc.astype(BF16)
    r = _sigmoid(jnp.dot(ucb, wr_ref[...].astype(BF16), preferred_element_type=F32) + br_ref[...])
    ig = _sigmoid(jnp.dot(ucb, wi_ref[...].astype(BF16), preferred_element_type=F32) + bi_ref[...])
    log_a = (-RG_C) * r * _softplus(-lam_ref[...])
    a = jnp.exp(log_a)
    b = jnp.sqrt(_neg_expm1(2.0 * log_a)) * (ig * uc)
    if hist:
        valid = t_idx >= hist
        a = jnp.where(valid, a, 1.0)
        b = jnp.where(valid, b, 0.0)
        b = jnp.where(t_idx == hist - 1, pltpu.roll(u, hist - 1, axis=0), b)
    d = 1
    while d < period:
        m = t_idx >= d
        b = jnp.where(m, a * pltpu.roll(b, d, axis=0), 0.0) + b
        a = jnp.where(m, a * pltpu.roll(a, d, axis=0), a)
        d *= 2
    h = b
    yg_ref[...] = (_gelu_tanh(gate_ref[...]) * h).astype(BF16)
    if hist:
        hl_ref[...] = h
    else:
        hl_ref[...] = h[rows - 1:rows, :]


def _rg_core(xw, nseq, period, hist, p, j):
    rows = period
    grid_b = nseq
    if hist:
        rows = nseq * period
        grid_b = 1
    c = RG_BLOCK
    vec = lambda name: pl.BlockSpec((None, 1, c), lambda b, n: (j, 0, n))
    in_specs = [
        pl.BlockSpec((rows, c), lambda b, n: (b, n)),
        pl.BlockSpec((rows, c), lambda b, n: (b, RG_BLOCKS + n)),
        pl.BlockSpec((None, CONV_K, c), lambda b, n: (j, 0, n)),
        vec("conv_b"),
        pl.BlockSpec((None, None, c, c), lambda b, n: (j, n, 0, 0)),
        vec("b_r"),
        pl.BlockSpec((None, None, c, c), lambda b, n: (j, n, 0, 0)),
        vec("b_i"),
        vec("lambda"),
    ]
    if hist:
        hl_spec = pl.BlockSpec((rows, c), lambda b, n: (b, n))
        hl_shape = jax.ShapeDtypeStruct((nseq * period, D_RNN), F32)
    else:
        hl_spec = pl.BlockSpec((None, 1, c), lambda b, n: (b, 0, n))
        hl_shape = jax.ShapeDtypeStruct((nseq, 1, D_RNN), F32)
    out_specs = [pl.BlockSpec((rows, c), lambda b, n: (b, n)), hl_spec]
    out_shape = [jax.ShapeDtypeStruct((nseq * period, D_RNN), BF16), hl_shape]
    return pl.pallas_call(
        functools.partial(_rg_core_kernel, period=period, hist=hist),
        grid=(grid_b, RG_BLOCKS),
        in_specs=in_specs,
        out_specs=out_specs,
        out_shape=out_shape,
        compiler_params=_cparams("arbitrary", "arbitrary"),
        name="rg_core",
    )(xw, xw, p["rg_conv_w"], _rows3(p["rg_conv_b"]), p["rg_w_rgate"], _rows3(p["rg_b_rgate"]),
      p["rg_w_igate"], _rows3(p["rg_b_igate"]), _rows3(p["rg_lambda"]))


def _rg_layer(xb_all, x_all, n_prompt_seq, t_prompt, n_sample_seq, t_sample, h0, buf, p, j, layer):
    n_p = n_prompt_seq * t_prompt
    xw = _proj(xb_all, p["rg_w_in"], j, 2 * D_RNN, tm=_row_tile(xb_all.shape[0]), tn=1024)
    yg_p, hl_p = _rg_core(xw, n_prompt_seq, t_prompt, 0, p, j)
    nb_p = _tail_rows(xw, n_prompt_seq, t_prompt, CONV_K - 1, D_RNN, 2 * D_RNN)
    hist = CONV_K
    period = hist + t_sample
    xs = xw[n_p:].reshape(n_sample_seq, t_sample, 2 * D_RNN)
    hist_u = jnp.concatenate([h0[:, None, :], buf], axis=1)
    hist_rows = jnp.concatenate([jnp.zeros_like(hist_u), hist_u], axis=-1)
    ext = jnp.concatenate([hist_rows, xs], axis=1)
    nb_s = ext[:, period - (CONV_K - 1):, D_RNN:]
    yg_s, h_s = _rg_core(ext.reshape(n_sample_seq * period, 2 * D_RNN), n_sample_seq, period, hist, p, j)
    yg_s = yg_s.reshape(n_sample_seq, period, D_RNN)[:, hist:].reshape(n_sample_seq * t_sample, D_RNN)
    hl_s = h_s.reshape(n_sample_seq, period, D_RNN)[:, period - 1]
    yg = jnp.concatenate([yg_p, yg_s], axis=0)
    z = _proj(yg, p["rg_w_out"], j, D_MODEL, tm=_row_tile(yg.shape[0]), tn=512)
    return z, (hl_p[:, 0, :], nb_p), (hl_s, nb_s)


def _row_tile(m):
    for t in (1088, 1024, 512, 256, 128, 64, 32, 16):
        if m % t == 0:
            return t
    raise ValueError(m)


def _row_tile_small(m):
    for t in (544, 512, 256, 128, 64, 32, 16):
        if m % t == 0:
            return t
    raise ValueError(m)


MOE_GROUPS = 8
MOE_PER_GROUP = 8
MOE_EXPERTS = MOE_GROUPS * MOE_PER_GROUP
MOE_D_FF = D_MODEL // 4
MOE_TM = 256
ROUTER_TM = 512


def _moe_num_tiles(n_tok):
    return (2 * n_tok) // MOE_TM + MOE_EXPERTS


def _first_index_of_max(v, lane):
    vmax = jnp.max(v, axis=-1, keepdims=True)
    idx = jnp.min(jnp.where(v == vmax, lane, v.shape[-1]), axis=-1, keepdims=True)
    return vmax, idx


def _router_kernel(z_ref, res_ref, lg_ref, lb_ref, wg_ref, bg_ref, we_ref, be_ref,
                   xo_ref, ids_ref, wts_ref, rank_ref, cnt_ref, carry_ref):
    i = pl.program_id(0)

    @pl.when(i == 0)
    def _():
        carry_ref[...] = jnp.zeros_like(carry_ref)

    xf = _layer_norm_rows(DN_ALPHA * res_ref[...] + z_ref[...], lg_ref[...], lb_ref[...])
    xo_ref[...] = xf
    tm = xf.shape[0]
    glog = jnp.dot(xf, wg_ref[...], precision=lax.Precision.HIGHEST, preferred_element_type=F32) + bg_ref[...]
    elog = jnp.dot(xf, we_ref[...], precision=lax.Precision.HIGHEST, preferred_element_type=F32) + be_ref[...]
    glane = lax.broadcasted_iota(I32, glog.shape, 1)
    gmax, gi = _first_index_of_max(glog, glane)
    gp = 1.0 / jnp.sum(jnp.exp(glog - gmax), axis=-1, keepdims=True)
    lane = lax.broadcasted_iota(I32, elog.shape, 1)
    in_group = (lane // MOE_PER_GROUP) == gi
    neg = jnp.float32(-jnp.inf)
    m1 = jnp.where(in_group, elog, neg)
    e1, i1 = _first_index_of_max(m1, lane)
    denom = jnp.sum(jnp.exp(m1 - e1), axis=-1, keepdims=True)
    m2 = jnp.where(lane == i1, neg, m1)
    e2, i2 = _first_index_of_max(m2, lane)
    p1 = 1.0 / denom
    p2 = jnp.exp(e2 - e1) / denom
    psum = p1 + p2
    ids_ref[...] = jnp.concatenate([i1, i2], axis=1)
    wts_ref[...] = jnp.concatenate([p1 / psum * gp, p2 / psum * gp], axis=1)

    r_i = lax.broadcasted_iota(I32, (tm, tm), 0)
    c_i = lax.broadcasted_iota(I32, (tm, tm), 1)
    before = (c_i < r_i).astype(BF16)
    oh1 = lane == i1
    oh2 = lane == i2
    carry = carry_ref[...]
    tot1 = jnp.sum(oh1.astype(F32), axis=0, keepdims=True)
    tot2 = jnp.sum(oh2.astype(F32), axis=0, keepdims=True)
    c1 = jnp.dot(before, oh1.astype(BF16), preferred_element_type=F32) + carry
    c2 = jnp.dot(before, oh2.astype(BF16), preferred_element_type=F32) + (carry + tot1)
    r1 = jnp.sum(jnp.where(oh1, c1, 0.0), axis=-1, keepdims=True)
    r2 = jnp.sum(jnp.where(oh2, c2, 0.0), axis=-1, keepdims=True)
    rank_ref[...] = jnp.concatenate([r1, r2], axis=1).astype(I32)
    carry = carry + tot1 + tot2
    carry_ref[...] = carry
    cnt_ref[...] = carry


def _moe_route(z, res, p, layer):
    n = z.shape[0]
    tm = ROUTER_TM if n % ROUTER_TM == 0 else n
    pair = lambda dt: jax.ShapeDtypeStruct((n, 2), dt)
    rows = pl.BlockSpec((tm, D_MODEL), lambda i: (i, 0))
    ln_row = pl.BlockSpec((None, 1, D_MODEL), lambda i: (layer, 0, 0))
    return pl.pallas_call(
        _router_kernel,
        grid=(n // tm,),
        in_specs=[
            rows,
            rows,
            ln_row,
            ln_row,
            pl.BlockSpec((None, D_MODEL, MOE_GROUPS), lambda i: (layer, 0, 0)),
            pl.BlockSpec((None, 1, MOE_GROUPS), lambda i: (layer, 0, 0)),
            pl.BlockSpec((None, D_MODEL, MOE_EXPERTS), lambda i: (layer, 0, 0)),
            pl.BlockSpec((None, 1, MOE_EXPERTS), lambda i: (layer, 0, 0)),
        ],
        out_specs=[
            rows,
            pl.BlockSpec((tm, 2), lambda i: (i, 0)),
            pl.BlockSpec((tm, 2), lambda i: (i, 0)),
            pl.BlockSpec((tm, 2), lambda i: (i, 0)),
            pl.BlockSpec((1, MOE_EXPERTS), lambda i: (0, 0)),
        ],
        out_shape=[jax.ShapeDtypeStruct((n, D_MODEL), F32), pair(I32), pair(F32), pair(I32),
                   jax.ShapeDtypeStruct((1, MOE_EXPERTS), F32)],
        scratch_shapes=[pltpu.VMEM((1, MOE_EXPERTS), F32)],
        compiler_params=_cparams("arbitrary"),
        name="moe_router",
    )(z, res, _rows3(p["ln_mix_g"]), _rows3(p["ln_mix_b"]), p["moe_router_group_w"],
      _rows3(p["moe_router_group_b"]), p["moe_router_expert_w"], _rows3(p["moe_router_expert_b"]))


def _moe_plan_kernel(cnt_ref, ids_ref, rank_ref, pos_ref, te_ref, nxt_ref, par_ref, nused_ref, *, n_tiles):
    cnt = cnt_ref[...]
    nt = jnp.floor((cnt + (MOE_TM - 1)) * (1.0 / MOE_TM))
    e_r = lax.broadcasted_iota(I32, (MOE_EXPERTS, MOE_EXPERTS), 0)
    e_c = lax.broadcasted_iota(I32, (MOE_EXPERTS, MOE_EXPERTS), 1)
    nt_b = nt.astype(BF16)
    first = jnp.dot(nt_b, (e_r < e_c).astype(BF16), preferred_element_type=F32)
    last = first + nt
    ids = ids_ref[...]
    lane = lax.broadcasted_iota(I32, (ids.shape[0], MOE_EXPERTS), 1)
    cols = []
    for k in range(2):
        sel = lane == ids[:, k:k + 1]
        cols.append(jnp.sum(jnp.where(sel, first, 0.0), axis=-1, keepdims=True))
    tile0 = jnp.concatenate(cols, axis=1).astype(I32)
    pos_ref[...] = tile0 * MOE_TM + rank_ref[...]
    last_col = jnp.sum(jnp.where(e_r == e_c, last, 0.0), axis=-1, keepdims=True)
    t_row = lax.broadcasted_iota(I32, (MOE_EXPERTS, n_tiles), 1).astype(F32)
    te = jnp.sum((last_col <= t_row).astype(F32), axis=0, keepdims=True)
    te = jnp.minimum(te, MOE_EXPERTS - 1.0)
    te_ref[...] = te.astype(I32)
    nt_col = jnp.sum(jnp.where(e_r == e_c, nt, 0.0), axis=-1, keepdims=True)
    e_col = lax.broadcasted_iota(I32, (MOE_EXPERTS, n_tiles), 0).astype(F32)
    nonempty = nt_col > 0.0
    below = jnp.sum(jnp.logical_and(e_col < te, nonempty).astype(F32), axis=0, keepdims=True)
    par_ref[...] = (below - 2.0 * jnp.floor(below * 0.5)).astype(I32)
    above = jnp.logical_and(e_col > te, nonempty)
    nxt_ref[...] = jnp.min(jnp.where(above, e_col, float(MOE_EXPERTS)), axis=0, keepdims=True).astype(I32)
    nused_ref[...] = jnp.sum(nt, axis=-1, keepdims=True).astype(I32)


def _moe_plan(cnt, ids, rank):
    n = ids.shape[0]
    tm = ROUTER_TM if n % ROUTER_TM == 0 else n
    n_tiles = _moe_num_tiles(n)
    return pl.pallas_call(
        functools.partial(_moe_plan_kernel, n_tiles=n_tiles),
        grid=(n // tm,),
        in_specs=[
            pl.BlockSpec((1, MOE_EXPERTS), lambda i: (0, 0)),
            pl.BlockSpec((tm, 2), lambda i: (i, 0)),
            pl.BlockSpec((tm, 2), lambda i: (i, 0)),
        ],
        out_specs=[
            pl.BlockSpec((tm, 2), lambda i: (i, 0)),
            pl.BlockSpec((1, n_tiles), lambda i: (0, 0)),
            pl.BlockSpec((1, n_tiles), lambda i: (0, 0)),
            pl.BlockSpec((1, n_tiles), lambda i: (0, 0)),
            pl.BlockSpec((1, 1), lambda i: (0, 0)),
        ],
        out_shape=[
            jax.ShapeDtypeStruct((n, 2), I32),
            jax.ShapeDtypeStruct((1, n_tiles), I32),
            jax.ShapeDtypeStruct((1, n_tiles), I32),
            jax.ShapeDtypeStruct((1, n_tiles), I32),
            jax.ShapeDtypeStruct((1, 1), I32),
        ],
        compiler_params=_cparams("arbitrary"),
        name="moe_plan",
    )(cnt, ids, rank)


def _row_copy(src_ref, src_row, dst_ref, dst_row, sem):
    return pltpu.make_async_copy(src_ref.at[pl.ds(src_row, 1)], dst_ref.at[pl.ds(dst_row, 1)], sem)


def _moe_dispatch_kernel(pos_ref, x_ref, xs_in_hbm, xs_hbm, sem):
    del xs_in_hbm
    tm = pos_ref.shape[0] // 2

    def start(n, c):
        _row_copy(x_ref, n, xs_hbm, pos_ref[2 * n], sem).start()
        _row_copy(x_ref, n, xs_hbm, pos_ref[2 * n + 1], sem).start()
        return c

    lax.fori_loop(0, tm, start, 0, unroll=8)

    def wait(n, c):
        _row_copy(x_ref, 0, xs_hbm, 0, sem).wait()
        return c

    lax.fori_loop(0, 2 * tm, wait, 0, unroll=8)


def _moe_dispatch(pos_flat, x, xs0):
    n = x.shape[0]
    n_rows = xs0.shape[0]
    tm = ROUTER_TM if n % ROUTER_TM == 0 else n
    return pl.pallas_call(
        _moe_dispatch_kernel,
        grid=(n // tm,),
        in_specs=[
            pl.BlockSpec((2 * tm,), lambda i: (i,), memory_space=pltpu.SMEM),
            pl.BlockSpec((tm, D_MODEL), lambda i: (i, 0)),
            pl.BlockSpec(memory_space=pl.ANY),
        ],
        out_specs=pl.BlockSpec(memory_space=pl.ANY),
        out_shape=jax.ShapeDtypeStruct((n_rows, D_MODEL), F32),
        scratch_shapes=[pltpu.SemaphoreType.DMA(())],
        input_output_aliases={2: 0},
        compiler_params=_cparams("arbitrary"),
        name="moe_dispatch",
    )(pos_flat, x, xs0)


def _moe_gmm_kernel(te_ref, nxt_ref, par_ref, nused_ref, xs_ref, wg_hbm, wu_hbm, wd_hbm, o_ref,
                    wg_buf, wu_buf, wd_buf, wgb_ref, wub_ref, wdb_ref, sems, *, layer):
    i = pl.program_id(0)
    used = i < nused_ref[0]
    e = te_ref[i]
    new_expert = jnp.logical_or(i == 0, e != te_ref[jnp.maximum(i - 1, 0)])

    def weight_copies(expert, slot):
        return [
            pltpu.make_async_copy(wg_hbm.at[layer, expert], wg_buf.at[slot], sems.at[0, slot]),
            pltpu.make_async_copy(wu_hbm.at[layer, expert], wu_buf.at[slot], sems.at[1, slot]),
            pltpu.make_async_copy(wd_hbm.at[layer, expert], wd_buf.at[slot], sems.at[2, slot]),
        ]

    @pl.when(jnp.logical_and(used, i == 0))
    def _():
        for c in weight_copies(e, par_ref[0]):
            c.start()

    @pl.when(jnp.logical_and(used, new_expert))
    def _():
        slot = par_ref[i]
        for c in weight_copies(e, slot):
            c.wait()
        nxt = nxt_ref[i]

        @pl.when(nxt < MOE_EXPERTS)
        def _():
            for c in weight_copies(nxt, 1 - slot):
                c.start()

        wgb_ref[...] = wg_buf[slot].astype(BF16)
        wub_ref[...] = wu_buf[slot].astype(BF16)
        wdb_ref[...] = wd_buf[slot].astype(BF16)

    @pl.when(used)
    def _():
        xb = xs_ref[...].astype(BF16)
        g = jnp.dot(xb, wgb_ref[...], preferred_element_type=F32)
        u = jnp.dot(xb, wub_ref[...], preferred_element_type=F32)
        h = (_silu(g) * u).astype(BF16)
        o_ref[...] = jnp.dot(h, wdb_ref[...], preferred_element_type=F32)

    @pl.when(jnp.logical_not(used))
    def _():
        o_ref[...] = jnp.zeros_like(o_ref)


def _moe_gmm(te, nxt, par, nused, xs, p, layer):
    n_rows = xs.shape[0]
    n_tiles = n_rows // MOE_TM
    hbm = pl.BlockSpec(memory_space=pl.ANY)
    return pl.pallas_call(
        functools.partial(_moe_gmm_kernel, layer=layer),
        grid_spec=pltpu.PrefetchScalarGridSpec(
            num_scalar_prefetch=4,
            grid=(n_tiles,),
            in_specs=[
                pl.BlockSpec((MOE_TM, D_MODEL), lambda i, te, nx, pa, nu: (jnp.minimum(i, nu[0] - 1), 0)),
                hbm,
                hbm,
                hbm,
            ],
            out_specs=pl.BlockSpec((MOE_TM, D_MODEL), lambda i, te, nx, pa, nu: (i, 0)),
            scratch_shapes=[
                pltpu.VMEM((2, D_MODEL, MOE_D_FF), F32),
                pltpu.VMEM((2, D_MODEL, MOE_D_FF), F32),
                pltpu.VMEM((2, MOE_D_FF, D_MODEL), F32),
                pltpu.VMEM((D_MODEL, MOE_D_FF), BF16),
                pltpu.VMEM((D_MODEL, MOE_D_FF), BF16),
                pltpu.VMEM((MOE_D_FF, D_MODEL), BF16),
                pltpu.SemaphoreType.DMA((3, 2)),
            ],
        ),
        out_shape=jax.ShapeDtypeStruct((n_rows, D_MODEL), F32),
        compiler_params=_cparams("arbitrary"),
        name="moe_gmm",
    )(te, nxt, par, nused, xs, p["moe_w_gate"], p["moe_w_up"], p["moe_w_down"])


def _moe_combine_ln_kernel(pos_ref, o_hbm, wts_ref, res_ref, g_ref, b_ref, xo_ref, xbo_ref, buf_ref, sem):
    tm = res_ref.shape[0]

    def start(n, c):
        _row_copy(o_hbm, pos_ref[2 * n], buf_ref.at[0], n, sem).start()
        _row_copy(o_hbm, pos_ref[2 * n + 1], buf_ref.at[1], n, sem).start()
        return c

    lax.fori_loop(0, tm, start, 0, unroll=8)

    def wait(n, c):
        _row_copy(o_hbm, 0, buf_ref.at[0], 0, sem).wait()
        return c

    lax.fori_loop(0, 2 * tm, wait, 0, unroll=8)
    w = wts_ref[...]
    y = w[:, 0:1] * buf_ref[0] + w[:, 1:2] * buf_ref[1]
    out = _layer_norm_rows(DN_ALPHA * res_ref[...] + y, g_ref[...], b_ref[...])
    xo_ref[...] = out
    xbo_ref[...] = out.astype(BF16)


def _moe_combine_ln(pos_flat, o, wts, res, g, b, layer):
    n = res.shape[0]
    tm = ROUTER_TM if n % ROUTER_TM == 0 else n
    return pl.pallas_call(
        _moe_combine_ln_kernel,
        grid=(n // tm,),
        in_specs=[
            pl.BlockSpec((2 * tm,), lambda i: (i,), memory_space=pltpu.SMEM),
            pl.BlockSpec(memory_space=pl.ANY),
            pl.BlockSpec((tm, 2), lambda i: (i, 0)),
            pl.BlockSpec((tm, D_MODEL), lambda i: (i, 0)),
            pl.BlockSpec((None, 1, D_MODEL), lambda i: (layer, 0, 0)),
            pl.BlockSpec((None, 1, D_MODEL), lambda i: (layer, 0, 0)),
        ],
        out_specs=[
            pl.BlockSpec((tm, D_MODEL), lambda i: (i, 0)),
            pl.BlockSpec((tm, D_MODEL), lambda i: (i, 0)),
        ],
        out_shape=[
            jax.ShapeDtypeStruct((n, D_MODEL), F32),
            jax.ShapeDtypeStruct((n, D_MODEL), BF16),
        ],
        scratch_shapes=[pltpu.VMEM((2, tm, D_MODEL), F32), pltpu.SemaphoreType.DMA(())],
        compiler_params=_cparams("arbitrary"),
        name="moe_combine_ln",
    )(pos_flat, o, wts, res, _rows3(g), _rows3(b))


def _moe_layer(z, res, sorted_buf, p, layer):
    n = z.shape[0]
    x, ids, wts, rank, cnt = _moe_route(z, res, p, layer)
    pos, te, nxt, par, nused = _moe_plan(cnt, ids, rank)
    pos_flat = pos.reshape(2 * n)
    xs = _moe_dispatch(pos_flat, x, sorted_buf)
    o = _moe_gmm(te.reshape(-1), nxt.reshape(-1), par.reshape(-1), nused.reshape(-1), xs, p, layer)
    x_new, xb_new = _moe_combine_ln(pos_flat, o, wts, x, p["ln_ffn_g"], p["ln_ffn_b"], layer)
    return x_new, xb_new, o


GDN_QK_HEADS = 16
GDN_V_HEADS = 32
GDN_HEAD = 128
GDN_QK_DIM = GDN_QK_HEADS * GDN_HEAD
GDN_V_DIM = GDN_V_HEADS * GDN_HEAD
GDN_CONV_DIM = 2 * GDN_QK_DIM + GDN_V_DIM
GDN_CHUNK = 64
GDN_HP_PROMPT = 4
GDN_HP_SAMPLE = 16
CONV_CB = 512


def _conv_act_kernel(u_ref, cw_ref, cb_ref, o_ref, *, period, masked, n_q_blocks, n_k_blocks):
    u = u_ref[...]
    rows, cblk = u.shape
    t_idx = lax.broadcasted_iota(I32, (rows, 1), 0) % period
    cw = cw_ref[...]
    acc = u * cw[CONV_K - 1:CONV_K, :]
    for s in range(1, CONV_K):
        acc = acc + _shift_rows(u, s, t_idx, masked) * cw[CONV_K - 1 - s:CONV_K - s, :]
    y = _silu(acc + cb_ref[...])
    if n_q_blocks or n_k_blocks:
        n = pl.program_id(1)
        is_q = n < n_q_blocks
        is_k = jnp.logical_and(n >= n_q_blocks, n < n_q_blocks + n_k_blocks)
        parts = []
        for hh in range(cblk // GDN_HEAD):
            yh = y[:, hh * GDN_HEAD:(hh + 1) * GDN_HEAD]
            rs = lax.rsqrt(jnp.sum(yh * yh, axis=-1, keepdims=True) + RMS_EPS)
            scale = jnp.where(is_q, rs * GDN_HEAD ** -0.5, jnp.where(is_k, rs, 1.0))
            parts.append(yh * scale)
        y = jnp.concatenate(parts, axis=1)
    o_ref[...] = y


def _conv_act(xw, nseq, period, masked, conv_w, conv_b, j, n_ch, *, n_q_blocks=0, n_k_blocks=0, col0=0):
    rows = period if masked else nseq * period
    grid_b = nseq if masked else 1
    c = CONV_CB
    cb0 = col0 // c
    return pl.pallas_call(
        functools.partial(_conv_act_kernel, period=period, masked=masked, n_q_blocks=n_q_blocks,
                          n_k_blocks=n_k_blocks),
        grid=(grid_b, n_ch // c),
        in_specs=[
            pl.BlockSpec((rows, c), lambda b, n: (b, cb0 + n)),
            pl.BlockSpec((None, CONV_K, c), lambda b, n: (j, 0, n)),
            pl.BlockSpec((None, 1, c), lambda b, n: (j, 0, n)),
        ],
        out_specs=pl.BlockSpec((rows, c), lambda b, n: (b, n)),
        out_shape=jax.ShapeDtypeStruct((nseq * period, n_ch), F32),
        compiler_params=_cparams("arbitrary", "arbitrary"),
        name="conv_act",
    )(xw, conv_w, conv_b)


def _scan_rows_add(g, t_idx, chunk):
    d = 1
    while d < chunk:
        g = g + jnp.where(t_idx >= d, pltpu.roll(g, d, axis=0), 0.0)
        d *= 2
    return g


def _gdn_gates_kernel(xb_ref, w_ref, alog_ref, dtb_ref, bg_ref, *, chunk):
    ba = jnp.dot(xb_ref[...], w_ref[...].astype(BF16), preferred_element_type=F32)
    b = ba[:, :GDN_V_HEADS]
    a = ba[:, GDN_V_HEADS:2 * GDN_V_HEADS]
    beta = _sigmoid(b)
    g = -jnp.exp(alog_ref[...]) * _softplus(a + dtb_ref[...])
    t_idx = lax.broadcasted_iota(I32, (g.shape[0], 1), 0) % chunk
    bg_ref[...] = jnp.concatenate([beta, _scan_rows_add(g, t_idx, chunk)], axis=1)


def _gdn_gates(xb, row0, n_rows, w_in, a_log, dt_bias, j, chunk):
    tm = _row_tile_small(n_rows)
    tm = tm if tm % chunk == 0 and row0 % tm == 0 else n_rows
    r0 = row0 // tm
    col_blk = (GDN_CONV_DIM + GDN_V_DIM) // LANES
    return pl.pallas_call(
        functools.partial(_gdn_gates_kernel, chunk=chunk),
        grid=(n_rows // tm,),
        in_specs=[
            pl.BlockSpec((tm, D_MODEL), lambda i: (r0 + i, 0)),
            pl.BlockSpec((None, D_MODEL, LANES), lambda i: (j, 0, col_blk)),
            pl.BlockSpec((None, 1, GDN_V_HEADS), lambda i: (j, 0, 0)),
            pl.BlockSpec((None, 1, GDN_V_HEADS), lambda i: (j, 0, 0)),
        ],
        out_specs=pl.BlockSpec((tm, 2 * GDN_V_HEADS), lambda i: (i, 0)),
        out_shape=jax.ShapeDtypeStruct((n_rows, 2 * GDN_V_HEADS), F32),
        compiler_params=_cparams("arbitrary"),
        name="gdn_gates",
    )(xb, w_in, _rows3(a_log), _rows3(dt_bias))


def _split_bf16(a):
    hi = a.astype(BF16)
    return hi, (a - hi.astype(F32)).astype(BF16)


def _dot3s(a_split, b_split, fused):
    a_hi, a_lo = a_split
    b_hi, b_lo = b_split
    d = functools.partial(jnp.dot, preferred_element_type=F32)
    if fused:
        return d(jnp.concatenate([a_hi, a_hi, a_lo], axis=1), jnp.concatenate([b_hi, b_lo, b_hi], axis=0))
    return d(a_hi, b_hi) + (d(a_hi, b_lo) + d(a_lo, b_hi))


def _dot_nt(a, b):
    return lax.dot_general(a, b, (((1,), (1,)), ((), ())), preferred_element_type=F32)


def _dot_tn(a, b):
    return lax.dot_general(a, b, (((0,), (0,)), ((), ())), preferred_element_type=F32)


def _gdn_chunk_kernel(*refs, L, hp, has_state):
    if has_state:
        q_ref, k_ref, v_ref, z_ref, bg_ref, nw_ref, s0_ref, yg_ref, s_ref = refs
    else:
        q_ref, k_ref, v_ref, z_ref, bg_ref, nw_ref, yg_ref, s_ref = refs
        s0_ref = None

    @pl.when(pl.program_id(2) == 0)
    def _():
        if has_state:
            s_ref[...] = s0_ref[...]
        else:
            s_ref[...] = jnp.zeros_like(s_ref)

    hd = GDN_HEAD
    heads = range(2 * hp)
    fused = L % 64 == 0
    bg = bg_ref[...]
    lane = lax.broadcasted_iota(I32, bg.shape, 1)
    r_i = lax.broadcasted_iota(I32, (L, L), 0)
    c_i = lax.broadcasted_iota(I32, (L, L), 1)
    causal = c_i <= r_i
    strict = c_i < r_i
    diag = c_i == r_i
    eye = diag.astype(F32)
    nw = nw_ref[...]
    h0 = pl.program_id(1) * (2 * hp)

    q = [q_ref[:, jp * hd:(jp + 1) * hd] for jp in range(hp)]
    k = [k_ref[:, jp * hd:(jp + 1) * hd] for jp in range(hp)]
    kb = [x.astype(BF16) for x in k]
    kk = [_dot_nt(kb[jp], kb[jp]) for jp in range(hp)]
    qk0 = [_dot_nt(q[jp].astype(BF16), kb[jp]) for jp in range(hp)]
    beta = [jnp.sum(jnp.where(lane == h0 + hv, bg, 0.0), axis=-1, keepdims=True) for hv in heads]
    gc = [jnp.sum(jnp.where(lane == GDN_V_HEADS + h0 + hv, bg, 0.0), axis=-1, keepdims=True) for hv in heads]
    g_row = [jnp.sum(jnp.where(diag, gc[hv], 0.0), axis=0, keepdims=True) for hv in heads]
    decay = [jnp.where(causal, jnp.exp(jnp.where(causal, gc[hv] - g_row[hv], 0.0)), 0.0) for hv in heads]
    a_mat = [jnp.where(strict, beta[hv] * kk[hv // 2] * decay[hv], 0.0) for hv in heads]
    t_inv = [eye - a_mat[hv] for hv in heads]
    sp = [_split_bf16(a_mat[hv]) for hv in heads]
    pw = [_dot3s(sp[hv], sp[hv], fused) for hv in heads]
    m = 2
    while m < L:
        sp = [_split_bf16(pw[hv]) for hv in heads]
        st = [_split_bf16(t_inv[hv]) for hv in heads]
        if 2 * m < L and fused:
            both = [_dot3s(tuple(jnp.concatenate([sp[hv][i], st[hv][i]], axis=0) for i in range(2)), sp[hv], fused)
                    for hv in heads]
            pw = [both[hv][:L] for hv in heads]
            t_inv = [t_inv[hv] + both[hv][L:] for hv in heads]
        else:
            if 2 * m < L:
                pw = [_dot3s(sp[hv], sp[hv], fused) for hv in heads]
            t_inv = [t_inv[hv] + _dot3s(st[hv], sp[hv], fused) for hv in heads]
        m *= 2
    e_g = [jnp.exp(gc[hv]) for hv in heads]
    g_last = [gc[hv][L - 1:L, :] for hv in heads]
    v = [v_ref[:, hv * hd:(hv + 1) * hd] for hv in heads]
    rhs = [jnp.concatenate([k[hv // 2] * (beta[hv] * e_g[hv]), v[hv] * beta[hv]], axis=1) for hv in heads]
    wu = [_dot3s(_split_bf16(t_inv[hv]), _split_bf16(rhs[hv]), fused) for hv in heads]
    s = [s_ref[hv] for hv in heads]
    sb = [x.astype(BF16) for x in s]
    wq = [jnp.concatenate([wu[hv][:, :hd].astype(BF16), (q[hv // 2] * e_g[hv]).astype(BF16)], axis=0)
          for hv in heads]
    r = [jnp.dot(wq[hv], sb[hv], preferred_element_type=F32) for hv in heads]
    ub = [(wu[hv][:, hd:] - r[hv][:L]).astype(BF16) for hv in heads]
    o = [r[hv][L:] + jnp.dot((qk0[hv // 2] * decay[hv]).astype(BF16), ub[hv], preferred_element_type=F32)
         for hv in heads]
    k_dec = [(k[hv // 2] * jnp.exp(g_last[hv] - gc[hv])).astype(BF16) for hv in heads]
    for hv in heads:
        s_ref[hv] = jnp.exp(g_last[hv]) * s[hv] + _dot_tn(k_dec[hv], ub[hv])
    for hv in heads:
        on = o[hv] * lax.rsqrt(jnp.mean(o[hv] * o[hv], axis=-1, keepdims=True) + RMS_EPS) * nw
        z = z_ref[:, hv * hd:(hv + 1) * hd]
        yg_ref[:, hv * hd:(hv + 1) * hd] = (on * _silu(z)).astype(BF16)


def _gdn_core(qkv, xw, bg, norm_w, j, s0, nseq, n_chunks, L, hp):
    hd = GDN_HEAD
    row = lambda b, g, n: b * n_chunks + n
    in_specs = [
        pl.BlockSpec((L, hp * hd), lambda b, g, n: (row(b, g, n), g)),
        pl.BlockSpec((L, hp * hd), lambda b, g, n: (row(b, g, n), GDN_QK_DIM // (hp * hd) + g)),
        pl.BlockSpec((L, 2 * hp * hd), lambda b, g, n: (row(b, g, n), 2 * GDN_QK_DIM // (2 * hp * hd) + g)),
        pl.BlockSpec((L, 2 * hp * hd), lambda b, g, n: (row(b, g, n), GDN_CONV_DIM // (2 * hp * hd) + g)),
        pl.BlockSpec((L, 2 * GDN_V_HEADS), lambda b, g, n: (row(b, g, n), 0)),
        pl.BlockSpec((None, 1, hd), lambda b, g, n: (j, 0, 0)),
    ]
    args = [qkv, qkv, qkv, xw, bg, _rows3(norm_w)]
    s_spec = pl.BlockSpec((None, 2 * hp, hd, hd), lambda b, g, n: (b, g, 0, 0))
    if s0 is not None:
        in_specs.append(s_spec)
        args.append(s0)
    return pl.pallas_call(
        functools.partial(_gdn_chunk_kernel, L=L, hp=hp, has_state=s0 is not None),
        grid=(nseq, GDN_QK_HEADS // hp, n_chunks),
        in_specs=in_specs,
        out_specs=[pl.BlockSpec((L, 2 * hp * hd), lambda b, g, n: (row(b, g, n), g)), s_spec],
        out_shape=[
            jax.ShapeDtypeStruct((nseq * n_chunks * L, GDN_V_DIM), BF16),
            jax.ShapeDtypeStruct((nseq, GDN_V_HEADS, hd, hd), F32),
        ],
        compiler_params=_cparams("arbitrary", "arbitrary", "arbitrary"),
        name="gdn_core",
    )(*args)


def _gdn_layer(xb_all, x_all, n_prompt_seq, t_prompt, n_sample_seq, t_sample, s0, buf, p, j, layer):
    n_p = n_prompt_seq * t_prompt
    n_s = n_sample_seq * t_sample
    n_main = GDN_CONV_DIM + GDN_V_DIM
    xw = _proj(xb_all, p["gdn_w_in"], j, n_main, tm=_row_tile(xb_all.shape[0]), tn=1024)
    w_ba = p["gdn_w_in"]
    cw, cb = p["gdn_conv_w"], jnp.zeros((p["gdn_conv_w"].shape[0], 1, GDN_CONV_DIM), F32)
    nqb = GDN_QK_DIM // CONV_CB
    qkv_p = _conv_act(xw, n_prompt_seq, t_prompt, True, cw, cb, j, GDN_CONV_DIM, n_q_blocks=nqb, n_k_blocks=nqb)
    nb_p = _tail_rows(xw, n_prompt_seq, t_prompt, CONV_K - 1, 0, GDN_CONV_DIM)
    chunk = math.gcd(t_prompt, GDN_CHUNK)
    bg_p = _gdn_gates(xb_all, 0, n_p, w_ba, p["gdn_a_log"], p["gdn_dt_bias"], j, chunk)
    yg_p, s_p = _gdn_core(qkv_p, xw, bg_p, p["gdn_norm_w"], j, None, n_prompt_seq, t_prompt // chunk, chunk,
                          GDN_HP_PROMPT)
    hist = CONV_K
    period = hist + t_sample
    xs = xw[n_p:].reshape(n_sample_seq, t_sample, n_main)
    hist_rows = jnp.concatenate([jnp.zeros((n_sample_seq, 1, GDN_CONV_DIM), F32), buf], axis=1)
    hist_rows = jnp.concatenate([hist_rows, jnp.zeros((n_sample_seq, hist, GDN_V_DIM), F32)], axis=-1)
    ext = jnp.concatenate([hist_rows, xs], axis=1)
    nb_s = ext[:, period - (CONV_K - 1):, :GDN_CONV_DIM]
    ext = ext.reshape(n_sample_seq * period, n_main)
    qkv_s = _conv_act(ext, n_sample_seq, period, False, cw, cb, j, GDN_CONV_DIM, n_q_blocks=nqb, n_k_blocks=nqb)
    bg_s = _gdn_gates(xb_all, n_p, n_s, w_ba, p["gdn_a_log"], p["gdn_dt_bias"], j, t_sample)
    bg_s = jnp.concatenate([jnp.zeros((n_sample_seq, hist, 2 * GDN_V_HEADS), F32),
                            bg_s.reshape(n_sample_seq, t_sample, 2 * GDN_V_HEADS)], axis=1)
    yg_s, s_s = _gdn_core(qkv_s, ext, bg_s.reshape(n_sample_seq * period, 2 * GDN_V_HEADS), p["gdn_norm_w"], j,
                          s0, n_sample_seq, 1, period, GDN_HP_SAMPLE)
    yg_s = yg_s.reshape(n_sample_seq, period, GDN_V_DIM)[:, hist:].reshape(n_s, GDN_V_DIM)
    yg = jnp.concatenate([yg_p, yg_s], axis=0)
    z = _proj(yg, p["gdn_w_out"], j, D_MODEL, tm=_row_tile(yg.shape[0]), tn=512)
    return z, (s_p, nb_p), (s_s, nb_s)


SSD_D_INNER = 2 * D_MODEL
SSD_HEAD_DIM = 64
SSD_HEADS = SSD_D_INNER // SSD_HEAD_DIM
SSD_GROUPS = 8
SSD_HPG = SSD_HEADS // SSD_GROUPS
SSD_STATE = 128
SSD_CONV_DIM = SSD_D_INNER + 2 * SSD_GROUPS * SSD_STATE
SSD_CHUNK = 128
SSD_GW = SSD_HPG * SSD_HEAD_DIM
SSD_NG_PROMPT = 2
SSD_NG_SAMPLE = 8


def _ssd_gates_kernel(xb_ref, w_ref, alog_ref, dtb_ref, da_ref, *, chunk):
    raw = jnp.dot(xb_ref[...], w_ref[...].astype(BF16), preferred_element_type=F32)[:, :SSD_HEADS]
    dt = _softplus(raw + dtb_ref[...])
    a = dt * (-jnp.exp(alog_ref[...]))
    t_idx = lax.broadcasted_iota(I32, (a.shape[0], 1), 0) % chunk
    da_ref[...] = jnp.concatenate([dt, _scan_rows_add(a, t_idx, chunk)], axis=1)


def _ssd_gates(xb, row0, n_rows, w_dt, a_log, dt_bias, j, chunk):
    tm = _row_tile_small(n_rows)
    tm = tm if tm % chunk == 0 and row0 % tm == 0 else n_rows
    r0 = row0 // tm
    col_blk = (SSD_D_INNER + SSD_CONV_DIM) // LANES
    return pl.pallas_call(
        functools.partial(_ssd_gates_kernel, chunk=chunk),
        grid=(n_rows // tm,),
        in_specs=[
            pl.BlockSpec((tm, D_MODEL), lambda i: (r0 + i, 0)),
            pl.BlockSpec((None, D_MODEL, LANES), lambda i: (j, 0, col_blk)),
            pl.BlockSpec((None, 1, SSD_HEADS), lambda i: (j, 0, 0)),
            pl.BlockSpec((None, 1, SSD_HEADS), lambda i: (j, 0, 0)),
        ],
        out_specs=pl.BlockSpec((tm, 2 * SSD_HEADS), lambda i: (i, 0)),
        out_shape=jax.ShapeDtypeStruct((n_rows, 2 * SSD_HEADS), F32),
        compiler_params=_cparams("arbitrary"),
        name="ssd_gates",
    )(xb, w_dt, _rows3(a_log), _rows3(dt_bias))


def _ssd_chunk_kernel(*refs, L, ng, has_state):
    if has_state:
        x_ref, b_ref, c_ref, z_ref, da_ref, dsk_ref, nw_ref, h0_ref, yg_ref, h_ref = refs
    else:
        x_ref, b_ref, c_ref, z_ref, da_ref, dsk_ref, nw_ref, yg_ref, h_ref = refs
        h0_ref = None

    @pl.when(pl.program_id(2) == 0)
    def _():
        if has_state:
            h_ref[...] = h0_ref[...]
        else:
            h_ref[...] = jnp.zeros_like(h_ref)

    g0 = pl.program_id(1) * ng
    da = da_ref[...]
    da_lane = lax.broadcasted_iota(I32, da.shape, 1)
    dsk = dsk_ref[...]
    dsk_lane = lax.broadcasted_iota(I32, dsk.shape, 1)
    r_i = lax.broadcasted_iota(I32, (L, L), 0)
    c_i = lax.broadcasted_iota(I32, (L, L), 1)
    causal = c_i <= r_i
    diag = c_i == r_i
    pd = 2 * SSD_HEAD_DIM
    st = SSD_STATE
    ppg = SSD_HPG // 2
    first_half = lax.broadcasted_iota(I32, (1, pd), 1) < SSD_HEAD_DIM
    top_half = lax.broadcasted_iota(I32, (pd, 1), 0) < SSD_HEAD_DIM
    groups = range(ng)
    pairs = range(ng * ppg)
    heads = range(ng * SSD_HPG)
    bmb = [b_ref[:, gi * st:(gi + 1) * st].astype(BF16) for gi in groups]
    cmb = [c_ref[:, gi * st:(gi + 1) * st].astype(BF16) for gi in groups]
    cb = [_dot_nt(cmb[gi], bmb[gi]) for gi in groups]
    head0 = g0 * SSD_HPG
    dt = [jnp.sum(jnp.where(da_lane == head0 + hh, da, 0.0), axis=-1, keepdims=True) for hh in heads]
    ac = [jnp.sum(jnp.where(da_lane == SSD_HEADS + head0 + hh, da, 0.0), axis=-1, keepdims=True) for hh in heads]
    ac_row = [jnp.sum(jnp.where(diag, ac[hh], 0.0), axis=0, keepdims=True) for hh in heads]
    decay = [jnp.where(causal, jnp.exp(jnp.where(causal, ac[hh] - ac_row[hh], 0.0)), 0.0) for hh in heads]
    m_h = [(cb[hh // SSD_HPG] * decay[hh]).astype(BF16) for hh in heads]
    ac_last = [ac[hh][L - 1:L, :] for hh in heads]
    dskip = [jnp.sum(jnp.where(dsk_lane == head0 + hh, dsk, 0.0), axis=-1, keepdims=True) for hh in heads]
    both = lambda vals, pr: jnp.where(first_half, vals[2 * pr], vals[2 * pr + 1])
    x = [x_ref[:, pr * pd:(pr + 1) * pd] for pr in pairs]
    xdt = [x[pr] * both(dt, pr) for pr in pairs]
    y_diag = [jnp.dot(m_h[2 * pr], jnp.where(first_half, xdt[pr], 0.0).astype(BF16), preferred_element_type=F32)
              + jnp.dot(m_h[2 * pr + 1], jnp.where(first_half, 0.0, xdt[pr]).astype(BF16),
                        preferred_element_type=F32) for pr in pairs]
    to_end = [jnp.exp(ac_last[hh] - ac[hh]) for hh in heads]
    states = [_dot_tn((xdt[pr] * both(to_end, pr)).astype(BF16), bmb[pr // ppg]) for pr in pairs]
    h_pair = [jnp.concatenate([h_ref[2 * pr], h_ref[2 * pr + 1]], axis=0) for pr in pairs]
    e_ac = [jnp.exp(ac[hh]) for hh in heads]
    y_off = [_dot_nt(cmb[pr // ppg], h_pair[pr].astype(BF16)) * both(e_ac, pr) for pr in pairs]
    cd = [jnp.exp(ac_last[hh]) for hh in heads]
    for pr in pairs:
        h_new = jnp.where(top_half, cd[2 * pr], cd[2 * pr + 1]) * h_pair[pr] + states[pr]
        h_ref[2 * pr] = h_new[:SSD_HEAD_DIM]
        h_ref[2 * pr + 1] = h_new[SSD_HEAD_DIM:]
    ys = [y_diag[pr] + y_off[pr] + both(dskip, pr) * x[pr] for pr in pairs]
    gw = SSD_GW
    for gi in groups:
        y = jnp.concatenate(ys[gi * ppg:(gi + 1) * ppg], axis=1) * _silu(z_ref[:, gi * gw:(gi + 1) * gw])
        yn = y * lax.rsqrt(jnp.mean(y * y, axis=-1, keepdims=True) + RMS_EPS) * nw_ref[:, gi * gw:(gi + 1) * gw]
        yg_ref[:, gi * gw:(gi + 1) * gw] = yn.astype(BF16)


def _ssd_core(xbc, xw, da, d_skip, norm_w, j, h0, nseq, n_chunks, L, ng):
    gw, st = ng * SSD_GW, ng * SSD_STATE
    row = lambda b, g, n: b * n_chunks + n
    in_specs = [
        pl.BlockSpec((L, gw), lambda b, g, n: (row(b, g, n), g)),
        pl.BlockSpec((L, st), lambda b, g, n: (row(b, g, n), SSD_D_INNER // st + g)),
        pl.BlockSpec((L, st), lambda b, g, n: (row(b, g, n), SSD_D_INNER // st + SSD_GROUPS // ng + g)),
        pl.BlockSpec((L, gw), lambda b, g, n: (row(b, g, n), g)),
        pl.BlockSpec((L, 2 * SSD_HEADS), lambda b, g, n: (row(b, g, n), 0)),
        pl.BlockSpec((None, 1, SSD_HEADS), lambda b, g, n: (j, 0, 0)),
        pl.BlockSpec((None, 1, gw), lambda b, g, n: (j, 0, g)),
    ]
    args = [xbc, xbc, xbc, xw, da, _rows3(d_skip), _rows3(norm_w)]
    h_spec = pl.BlockSpec((None, ng * SSD_HPG, SSD_HEAD_DIM, SSD_STATE), lambda b, g, n: (b, g, 0, 0))
    if h0 is not None:
        in_specs.append(h_spec)
        args.append(h0)
    return pl.pallas_call(
        functools.partial(_ssd_chunk_kernel, L=L, ng=ng, has_state=h0 is not None),
        grid=(nseq, SSD_GROUPS // ng, n_chunks),
        in_specs=in_specs,
        out_specs=[pl.BlockSpec((L, gw), lambda b, g, n: (row(b, g, n), g)), h_spec],
        out_shape=[
            jax.ShapeDtypeStruct((nseq * n_chunks * L, SSD_D_INNER), BF16),
            jax.ShapeDtypeStruct((nseq, SSD_HEADS, SSD_HEAD_DIM, SSD_STATE), F32),
        ],
        compiler_params=_cparams("arbitrary", "arbitrary", "arbitrary"),
        name="ssd_core",
    )(*args)


def _ssd_layer(xb_all, x_all, n_prompt_seq, t_prompt, n_sample_seq, t_sample, h0, buf, p, j, layer):
    n_p = n_prompt_seq * t_prompt
    n_s = n_sample_seq * t_sample
    n_main = SSD_D_INNER + SSD_CONV_DIM
    xw = _proj(xb_all, p["ssd_w_in"], j, n_main, tm=_row_tile(xb_all.shape[0]), tn=1024)
    w_dt = p["ssd_w_in"]
    cw, cb = p["ssd_conv_w"], _rows3(p["ssd_conv_b"])
    xbc_p = _conv_act(xw, n_prompt_seq, t_prompt, True, cw, cb, j, SSD_CONV_DIM, col0=SSD_D_INNER)
    nb_p = _tail_rows(xw, n_prompt_seq, t_prompt, CONV_K - 1, SSD_D_INNER, n_main)
    chunk = math.gcd(t_prompt, SSD_CHUNK)
    da_p = _ssd_gates(xb_all, 0, n_p, w_dt, p["ssd_a_log"], p["ssd_dt_bias"], j, chunk)
    yg_p, h_p = _ssd_core(xbc_p, xw, da_p, p["ssd_d"], p["ssd_norm_w"], j, None, n_prompt_seq, t_prompt // chunk,
                          chunk, SSD_NG_PROMPT)
    hist = CONV_K
    period = hist + t_sample
    xs = xw[n_p:].reshape(n_sample_seq, t_sample, n_main)
    hist_rows = jnp.concatenate([jnp.zeros((n_sample_seq, 1, SSD_CONV_DIM), F32), buf], axis=1)
    hist_rows = jnp.concatenate([jnp.zeros((n_sample_seq, hist, SSD_D_INNER), F32), hist_rows], axis=-1)
    ext = jnp.concatenate([hist_rows, xs], axis=1)
    nb_s = ext[:, period - (CONV_K - 1):, SSD_D_INNER:]
    ext = ext.reshape(n_sample_seq * period, n_main)
    xbc_s = _conv_act(ext, n_sample_seq, period, False, cw, cb, j, SSD_CONV_DIM, col0=SSD_D_INNER)
    da_s = _ssd_gates(xb_all, n_p, n_s, w_dt, p["ssd_a_log"], p["ssd_dt_bias"], j, t_sample)
    da_s = jnp.concatenate([jnp.zeros((n_sample_seq, hist, 2 * SSD_HEADS), F32),
                            da_s.reshape(n_sample_seq, t_sample, 2 * SSD_HEADS)], axis=1)
    yg_s, h_s = _ssd_core(xbc_s, ext, da_s.reshape(n_sample_seq * period, 2 * SSD_HEADS), p["ssd_d"],
                          p["ssd_norm_w"], j, h0, n_sample_seq, 1, period, SSD_NG_SAMPLE)
    yg_s = yg_s.reshape(n_sample_seq, period, SSD_D_INNER)[:, hist:].reshape(n_s, SSD_D_INNER)
    yg = jnp.concatenate([yg_p, yg_s], axis=0)
    z = _proj(yg, p["ssd_w_out"], j, D_MODEL, tm=_row_tile(yg.shape[0]), tn=512)
    return z, (h_p, nb_p), (h_s, nb_s)


N_MIXERS = 3


def kernel(x_prompt, x_sample, state_rg_h, state_rg_conv, state_gdn_s, state_gdn_conv, state_ssd_h,
           state_ssd_conv, ln_mix_g, ln_mix_b, ln_ffn_g, ln_ffn_b, moe_router_group_w, moe_router_group_b,
           moe_router_expert_w, moe_router_expert_b, moe_w_gate, moe_w_up, moe_w_down, rg_w_in, rg_conv_w,
           rg_conv_b, rg_w_rgate, rg_b_rgate, rg_w_igate, rg_b_igate, rg_lambda, rg_w_out, gdn_w_in, gdn_conv_w,
           gdn_a_log, gdn_dt_bias, gdn_norm_w, gdn_w_out, ssd_w_in, ssd_conv_w, ssd_conv_b, ssd_a_log,
           ssd_dt_bias, ssd_d, ssd_norm_w, ssd_w_out):
    p = dict(
        ln_mix_g=ln_mix_g, ln_mix_b=ln_mix_b, ln_ffn_g=ln_ffn_g, ln_ffn_b=ln_ffn_b,
        moe_router_group_w=moe_router_group_w, moe_router_group_b=moe_router_group_b,
        moe_router_expert_w=moe_router_expert_w, moe_router_expert_b=moe_router_expert_b,
        moe_w_gate=moe_w_gate, moe_w_up=moe_w_up, moe_w_down=moe_w_down,
        rg_w_in=rg_w_in, rg_conv_w=rg_conv_w, rg_conv_b=rg_conv_b, rg_w_rgate=rg_w_rgate,
        rg_b_rgate=rg_b_rgate, rg_w_igate=rg_w_igate, rg_b_igate=rg_b_igate, rg_lambda=rg_lambda,
        rg_w_out=rg_w_out,
        gdn_w_in=gdn_w_in, gdn_conv_w=gdn_conv_w, gdn_a_log=gdn_a_log, gdn_dt_bias=gdn_dt_bias,
        gdn_norm_w=gdn_norm_w, gdn_w_out=gdn_w_out,
        ssd_w_in=ssd_w_in, ssd_conv_w=ssd_conv_w, ssd_conv_b=ssd_conv_b, ssd_a_log=ssd_a_log,
        ssd_dt_bias=ssd_dt_bias, ssd_d=ssd_d, ssd_norm_w=ssd_norm_w, ssd_w_out=ssd_w_out,
    )
    bp, tp, _ = x_prompt.shape
    bs, ts, _ = x_sample.shape
    n_p = bp * tp
    x = jnp.concatenate([x_prompt.reshape(n_p, D_MODEL), x_sample.reshape(bs * ts, D_MODEL)], axis=0)
    xb = x.astype(BF16)
    sorted_buf = jnp.zeros((_moe_num_tiles(x.shape[0]) * MOE_TM, D_MODEL), F32)
    layers = {0: (_rg_layer, state_rg_h, state_rg_conv), 1: (_gdn_layer, state_gdn_s, state_gdn_conv),
              2: (_ssd_layer, state_ssd_h, state_ssd_conv)}
    new_p = {0: ([], []), 1: ([], []), 2: ([], [])}
    new_s = {0: ([], []), 1: ([], []), 2: ([], [])}
    for i in range(DEPTH):
        kind, j = i % N_MIXERS, i // N_MIXERS
        fn, st, cv = layers[kind]
        z, (st_p, cv_p), (st_s, cv_s) = fn(xb, x, bp, tp, bs, ts, st[j], cv[j], p, j, i)
        new_p[kind][0].append(st_p)
        new_p[kind][1].append(cv_p)
        new_s[kind][0].append(st_s)
        new_s[kind][1].append(cv_s)
        x, xb, sorted_buf = _moe_layer(z, x, sorted_buf, p, i)
    outs = [x[:n_p].reshape(bp, tp, D_MODEL), x[n_p:].reshape(bs, ts, D_MODEL)]
    for kind in range(N_MIXERS):
        for which in range(2):
            outs.append(jnp.stack(new_p[kind][which]))
            outs.append(jnp.stack(new_s[kind][which]))
    return tuple(outs)
```
